```python
import math
import jax
import jax.numpy as jnp
from jax import lax
import numpy as np

D_MODEL = 4096
BATCH = 16
SEQ = 256
DEPTH = 2
DEC_BATCH = 8
DEC_SEQ = 4096
PAST_LEN = 256

GRID_W = 64
EPS = 1e-6
CONV_W = 4
CONV_LEFT = 1
D_LRU = 1024
LRU_BLOCKS = 8
LRU_BW = D_LRU // LRU_BLOCKS
LRU_C = 8.0
D_SSD = 2048
SSD_HEAD_DIM = 64
SSD_HEADS = D_SSD // SSD_HEAD_DIM
SSD_GROUPS = 4
SSD_STATE = 128
SSD_CHUNK = 128
D_XBC = D_SSD + 2 * SSD_GROUPS * SSD_STATE
D_POOL = 1024
POOL_WINDOWS = (2, 4, 8, 16)
POOL_GROUP = D_POOL // len(POOL_WINDOWS)
N_BRANCH = 3
SPLIT_SIZES = (D_LRU, D_LRU, D_SSD, D_XBC, 2 * SSD_HEADS, D_POOL, N_BRANCH * D_MODEL)
N_IN = D_LRU + D_LRU + D_SSD + D_XBC + 2 * SSD_HEADS + D_POOL + N_BRANCH * D_MODEL
N_EXPERTS = 16
N_EXPERT_GROUPS = 4
EXPERTS_PER_GROUP = N_EXPERTS // N_EXPERT_GROUPS
TOP_K = 2
D_EXPERT = 1024

kernel_name = 'hybrid_lru_ssd_pool_moe_diffusion_step'


def rmsnorm(x, g):
    xf = x.astype(jnp.float32)
    y = xf * lax.rsqrt(jnp.mean(xf * xf, axis=-1, keepdims=True) + EPS)
    return (y * g.astype(jnp.float32)).astype(x.dtype)


def grid_pos_embed(n_tokens, dtype):
    rows = n_tokens // GRID_W
    row = jnp.repeat(jnp.arange(rows), GRID_W).astype(jnp.float32)
    col = jnp.tile(jnp.arange(GRID_W), rows).astype(jnp.float32)
    quarter = D_MODEL // 4
    omega = 1.0 / (10000.0 ** (jnp.arange(quarter, dtype=jnp.float32) / quarter))
    ang_r = row[:, None] * omega[None, :]
    ang_c = col[:, None] * omega[None, :]
    emb = jnp.concatenate([jnp.sin(ang_r), jnp.cos(ang_r), jnp.sin(ang_c), jnp.cos(ang_c)], axis=-1)
    return emb.astype(dtype)


def conv_centred(x, w, b):
    L = x.shape[1]
    xp = jnp.pad(x, ((0, 0), (CONV_LEFT, CONV_W - 1 - CONV_LEFT), (0, 0)))
    y = b + xp[:, 0:L] * w[0]
    for k in range(1, CONV_W):
        y = y + xp[:, k:k + L] * w[k]
    return y


def block_diag(x, w, b):
    xb = x.reshape(x.shape[0], x.shape[1], LRU_BLOCKS, LRU_BW)
    return jnp.einsum('blni,nij->blnj', xb, w).reshape(x.shape) + b


def linear_scan(a, bx, h0, reverse):
    def step(h, ab):
        at, bt = ab
        h = at * h + bt
        return h, h
    hT, hs = lax.scan(step, h0, (jnp.swapaxes(a, 0, 1), jnp.swapaxes(bx, 0, 1)), reverse=reverse)
    return jnp.swapaxes(hs, 0, 1), hT


def rglru_branch(xa, ga, p, h0):
    xc = conv_centred(xa, p['conv_a_w'], p['conv_a_b']).astype(jnp.float32)
    y = jnp.zeros_like(xc)
    finals = []
    for d in range(2):
        r = jax.nn.sigmoid(block_diag(xc, p['w_r'][d], p['b_r'][d]))
        i = jax.nn.sigmoid(block_diag(xc, p['w_i'][d], p['b_i'][d]))
        log_a = -LRU_C * r * jax.nn.softplus(-p['lru_lambda'][d].astype(jnp.float32))
        a = jnp.exp(log_a)
        bx = jnp.sqrt(-jnp.expm1(2.0 * log_a)) * (i * xc)
        hs, hT = linear_scan(a, bx, h0[:, d].astype(jnp.float32), reverse=(d == 1))
        y = y + hs
        finals.append(hT)
    out = y * jax.nn.gelu(ga.astype(jnp.float32))
    return out.astype(xa.dtype), jnp.stack(finals, axis=1)


def ssd_scan(x, dt, A, Bm, Cm, h0):
    b, L, H, P = x.shape
    G, N = Bm.shape[2], Bm.shape[3]
    R = H // G
    Q = SSD_CHUNK
    nc = L // Q
    xdt = (x * dt[..., None]).reshape(b, nc, Q, G, R, P)
    a_cs = jnp.cumsum((dt * A).reshape(b, nc, Q, G, R), axis=2)
    Bc = Bm.reshape(b, nc, Q, G, N)
    Cc = Cm.reshape(b, nc, Q, G, N)
    causal = jnp.tril(jnp.ones((Q, Q), dtype=bool))[None, None, :, :, None, None]
    seg = a_cs[:, :, :, None] - a_cs[:, :, None, :]
    decay = jnp.exp(jnp.where(causal, seg, -jnp.inf))
    scores = jnp.einsum('bcign,bcjgn->bcijg', Cc, Bc)
    y_diag = jnp.einsum('bcijg,bcijgr,bcjgrp->bcigrp', scores, decay, xdt)
    decay_to_end = jnp.exp(a_cs[:, :, -1:] - a_cs)
    chunk_states = jnp.einsum('bcjgn,bcjgr,bcjgrp->bcgrpn', Bc, decay_to_end, xdt)
    chunk_decay = jnp.exp(a_cs[:, :, -1])

    def step(h, inp):
        dec, st = inp
        return dec[..., None, None] * h + st, h

    hT, h_in = lax.scan(step, h0.astype(jnp.float32).reshape(b, G, R, P, N),
                        (jnp.moveaxis(chunk_decay, 1, 0), jnp.moveaxis(chunk_states, 1, 0)))
    h_in = jnp.moveaxis(h_in, 0, 1)
    y_off = jnp.einsum('bcign,bcgrpn,bcigr->bcigrp', Cc, h_in, jnp.exp(a_cs))
    y = (y_diag + y_off).reshape(b, L, H, P)
    return y, hT.reshape(b, H, P, N)


def ssd_branch(zb, xbc, dtb, p, h0):
    b, L, _ = zb.shape
    xbc = jax.nn.silu(conv_centred(xbc, p['conv_b_w'], p['conv_b_b'])).astype(jnp.float32)
    gn = SSD_GROUPS * SSD_STATE
    xs = xbc[..., :D_SSD].reshape(b, L, SSD_HEADS, SSD_HEAD_DIM)
    Bm = xbc[..., D_SSD:D_SSD + gn].reshape(b, L, SSD_GROUPS, SSD_STATE)
    Cm = xbc[..., D_SSD + gn:].reshape(b, L, SSD_GROUPS, SSD_STATE)
    y = p['d_skip'].astype(jnp.float32)[:, None] * xs
    finals = []
    for d in range(2):
        dt = jax.nn.softplus(dtb[..., d * SSD_HEADS:(d + 1) * SSD_HEADS].astype(jnp.float32)
                             + p['dt_bias'][d].astype(jnp.float32))
        A = -jnp.exp(p['a_log'][d].astype(jnp.float32))
        if d == 0:
            yd, hT = ssd_scan(xs, dt, A, Bm, Cm, h0[:, 0])
        else:
            yd, hT = ssd_scan(jnp.flip(xs, 1), jnp.flip(dt, 1), A, jnp.flip(Bm, 1), jnp.flip(Cm, 1), h0[:, 1])
            yd = jnp.flip(yd, 1)
        y = y + yd
        finals.append(hT)
    y = y.reshape(b, L, D_SSD) * jax.nn.silu(zb.astype(jnp.float32))
    yg = y.reshape(b, L, SSD_GROUPS, D_SSD // SSD_GROUPS)
    yg = yg * lax.rsqrt(jnp.mean(yg * yg, axis=-1, keepdims=True) + EPS)
    y = yg.reshape(b, L, D_SSD) * p['ssd_norm_g'].astype(jnp.float32)
    return y, jnp.stack(finals, axis=1)


def pool_branch(xc, p):
    b, L, _ = xc.shape
    xf = xc.astype(jnp.float32)
    S = jnp.concatenate([jnp.zeros((b, 1, D_POOL), jnp.float32), jnp.cumsum(xf, axis=1)], axis=1)
    t = np.arange(L)
    outs = []
    for k, w in enumerate(POOL_WINDOWS):
        lo = np.clip(t - w // 2, 0, L)
        hi = np.clip(t + w // 2, 0, L)
        cnt = jnp.asarray((hi - lo).astype(np.float32))
        c0, c1 = k * POOL_GROUP, (k + 1) * POOL_GROUP
        mean = (S[:, hi, c0:c1] - S[:, lo, c0:c1]) / cnt[:, None]
        outs.append((mean - xf[:, :, c0:c1]) @ p['w_pool'][k])
    return jnp.concatenate(outs, axis=-1) * p['pool_scale']


def mixer(u, p, lru_h0, ssd_h0):
    b, L, _ = u.shape
    proj = u @ p['w_in']
    points = np.cumsum(SPLIT_SIZES)[:-1].tolist()
    xa, ga, zb, xbc, dtb, xc, gate_logits = jnp.split(proj, points, axis=-1)
    ya, lru_T = rglru_branch(xa, ga, p, lru_h0)
    yb, ssd_T = ssd_branch(zb, xbc, dtb, p, ssd_h0)
    yc = pool_branch(xc, p)
    gates = jax.nn.sigmoid(gate_logits.astype(jnp.float32)).reshape(b, L, N_BRANCH, D_MODEL)
    merged = (gates[:, :, 0] * (ya @ p['w_oa'])
              + gates[:, :, 1] * (yb @ p['w_ob'])
              + gates[:, :, 2] * (yc @ p['w_oc']))
    return merged.astype(u.dtype) @ p['w_out'], lru_T, ssd_T


def moe(u, p, w_router, b_router):
    b, L, D = u.shape
    T = b * L
    xt = u.reshape(T, D)
    s = jax.nn.sigmoid((xt @ w_router).astype(jnp.float32))
    sel = (s + b_router.astype(jnp.float32)).reshape(T, N_EXPERT_GROUPS, EXPERTS_PER_GROUP)
    group_score = lax.top_k(sel, TOP_K)[0].sum(-1)
    g = jnp.argmax(group_score, axis=-1)
    sel_in = jnp.sum(sel * jax.nn.one_hot(g, N_EXPERT_GROUPS, dtype=jnp.float32)[..., None], axis=1)
    _, local = lax.top_k(sel_in, TOP_K)
    idx = g[:, None] * EXPERTS_PER_GROUP + local
    w = jnp.take_along_axis(s, idx, axis=1)
    w = w / jnp.sum(w, axis=-1, keepdims=True)
    gate = jnp.sum(jax.nn.one_hot(idx, N_EXPERTS, dtype=jnp.float32) * w[..., None], axis=1)
    out = jnp.zeros((T, D), jnp.float32)
    for e in range(N_EXPERTS):
        h = jax.nn.silu(xt @ p['w_gate'][e]) * (xt @ p['w_up'][e])
        out = out + gate[:, e:e + 1] * (h @ p['w_down'][e])
    return out.astype(u.dtype).reshape(b, L, D)


def layer(x, cond, p, w_router, b_router, lru_h0, ssd_h0):
    mod = (jax.nn.silu(cond) @ p['w_mod'] + p['b_mod'])[:, None, :]
    sh1, sc1, g1, sh2, sc2, g2 = jnp.split(mod, 6, axis=-1)
    u = rmsnorm(x, p['norm1_g']) * (1 + sc1) + sh1
    m, lru_T, ssd_T = mixer(u, p, lru_h0, ssd_h0)
    x = x + g1 * m
    u = rmsnorm(x, p['norm2_g']) * (1 + sc2) + sh2
    x = x + g2 * moe(u, p, w_router, b_router)
    return x, lru_T, ssd_T


def setup_inputs(seed: int = 0) -> dict:
    key = jax.random.key(seed)
    keys = jax.random.split(key, 48)
    counter = [0]
    f32 = jnp.float32

    def nxt():
        k = keys[counter[0]]
        counter[0] += 1
        return k

    def nrm(shape, scale):
        return jax.random.normal(nxt(), shape, f32) * scale

    def unif(shape, lo, hi):
        return jax.random.uniform(nxt(), shape, f32, lo, hi)

    D = D_MODEL
    H, P, N = SSD_HEADS, SSD_HEAD_DIM, SSD_STATE
    x_prompt = nrm((BATCH, SEQ, D), 1.0)
    x_sample = nrm((DEC_BATCH, DEC_SEQ, D), 1.0)
    state_lru = nrm((DEC_BATCH, DEPTH, 2, D_LRU), 0.5)
    state_ssd = nrm((DEC_BATCH, DEPTH, 2, H, P, N), 0.1)
    c = nrm((DEC_BATCH, D), 1.0)
    c_ctx = nrm((D,), 1.0)
    w_mod = nrm((DEPTH, D, 6 * D), 0.5 * D ** -0.5)
    b_mod = nrm((DEPTH, 6 * D), 0.02)
    norm1_g = 1.0 + nrm((DEPTH, D), 0.02)
    w_in = nrm((DEPTH, D, N_IN), D ** -0.5)
    conv_a_w = nrm((DEPTH, CONV_W, D_LRU), CONV_W ** -0.5)
    conv_a_b = nrm((DEPTH, D_LRU), 0.02)
    w_r = nrm((DEPTH, 2, LRU_BLOCKS, LRU_BW, LRU_BW), LRU_BW ** -0.5)
    b_r = nrm((DEPTH, 2, D_LRU), 0.02)
    w_i = nrm((DEPTH, 2, LRU_BLOCKS, LRU_BW, LRU_BW), LRU_BW ** -0.5)
    b_i = nrm((DEPTH, 2, D_LRU), 0.02)
    a0 = unif((DEPTH, 2, D_LRU), 0.9, 0.999) ** (1.0 / LRU_C)
    lru_lambda = jnp.log(a0) - jnp.log1p(-a0)
    w_oa = nrm((DEPTH, D_LRU, D), D_LRU ** -0.5)
    conv_b_w = nrm((DEPTH, CONV_W, D_XBC), CONV_W ** -0.5)
    conv_b_b = nrm((DEPTH, D_XBC), 0.02)
    dt0 = jnp.exp(unif((DEPTH, 2, H), math.log(1e-3), math.log(1e-1)))
    dt_bias = dt0 + jnp.log(-jnp.expm1(-dt0))
    a_log = jnp.log(unif((DEPTH, 2, H), 1.0, 16.0))
    d_skip = 1.0 + nrm((DEPTH, H), 0.1)
    ssd_norm_g = 1.0 + nrm((DEPTH, D_SSD), 0.02)
    w_ob = nrm((DEPTH, D_SSD, D), D_SSD ** -0.5)
    w_pool = nrm((DEPTH, len(POOL_WINDOWS), POOL_GROUP, POOL_GROUP), POOL_GROUP ** -0.5)
    pool_scale = 1.0 + nrm((DEPTH, D_POOL), 0.1)
    w_oc = nrm((DEPTH, D_POOL, D), D_POOL ** -0.5)
    w_out = nrm((DEPTH, D, D), D ** -0.5)
    norm2_g = 1.0 + nrm((DEPTH, D), 0.02)
    w_router = nrm((D, N_EXPERTS), D ** -0.5)
    b_router = nrm((N_EXPERTS,), 0.01)
    w_gate = nrm((DEPTH, N_EXPERTS, D, D_EXPERT), D ** -0.5)
    w_up = nrm((DEPTH, N_EXPERTS, D, D_EXPERT), D ** -0.5)
    w_down = nrm((DEPTH, N_EXPERTS, D_EXPERT, D), D_EXPERT ** -0.5)
    final_g = 1.0 + nrm((D,), 0.02)
    return {'x_prompt': x_prompt, 'x_sample': x_sample, 'state_lru': state_lru, 'state_ssd': state_ssd,
            'c': c, 'c_ctx': c_ctx, 'w_mod': w_mod, 'b_mod': b_mod, 'norm1_g': norm1_g, 'w_in': w_in,
            'conv_a_w': conv_a_w, 'conv_a_b': conv_a_b, 'w_r': w_r, 'b_r': b_r, 'w_i': w_i, 'b_i': b_i,
            'lru_lambda': lru_lambda, 'w_oa': w_oa, 'conv_b_w': conv_b_w, 'conv_b_b': conv_b_b,
            'dt_bias': dt_bias, 'a_log': a_log, 'd_skip': d_skip, 'ssd_norm_g': ssd_norm_g, 'w_ob': w_ob,
            'w_pool': w_pool, 'pool_scale': pool_scale, 'w_oc': w_oc, 'w_out': w_out, 'norm2_g': norm2_g,
            'w_router': w_router, 'b_router': b_router, 'w_gate': w_gate, 'w_up': w_up, 'w_down': w_down,
            'final_g': final_g}


def reference(x_prompt, x_sample, state_lru, state_ssd, c, c_ctx, w_mod, b_mod, norm1_g, w_in,
              conv_a_w, conv_a_b, w_r, b_r, w_i, b_i, lru_lambda, w_oa, conv_b_w, conv_b_b,
              dt_bias, a_log, d_skip, ssd_norm_g, w_ob, w_pool, pool_scale, w_oc, w_out, norm2_g,
              w_router, b_router, w_gate, w_up, w_down, final_g):
    bp = x_prompt.shape[0]
    lru0 = jnp.zeros((bp, 2, D_LRU), jnp.float32)
    ssd0 = jnp.zeros((bp, 2, SSD_HEADS, SSD_HEAD_DIM, SSD_STATE), jnp.float32)
    xp = x_prompt
    xs = x_sample + grid_pos_embed(x_sample.shape[1], x_sample.dtype)[None]
    lru_states = []
    ssd_states = []
    for l in range(DEPTH):
        p = dict(w_mod=w_mod[l], b_mod=b_mod[l], norm1_g=norm1_g[l], w_in=w_in[l],
                 conv_a_w=conv_a_w[l], conv_a_b=conv_a_b[l], w_r=w_r[l], b_r=b_r[l], w_i=w_i[l],
                 b_i=b_i[l], lru_lambda=lru_lambda[l], w_oa=w_oa[l], conv_b_w=conv_b_w[l],
                 conv_b_b=conv_b_b[l], dt_bias=dt_bias[l], a_log=a_log[l], d_skip=d_skip[l],
                 ssd_norm_g=ssd_norm_g[l], w_ob=w_ob[l], w_pool=w_pool[l], pool_scale=pool_scale[l],
                 w_oc=w_oc[l], w_out=w_out[l], norm2_g=norm2_g[l], w_gate=w_gate[l], w_up=w_up[l],
                 w_down=w_down[l])
        xp, lru_T, ssd_T = layer(xp, c_ctx[None, :], p, w_router, b_router, lru0, ssd0)
        lru_states.append(lru_T)
        ssd_states.append(ssd_T)
        xs, _, _ = layer(xs, c, p, w_router, b_router, state_lru[:, l], state_ssd[:, l])
    y_prompt = rmsnorm(xp, final_g)
    y_sample = rmsnorm(xs, final_g)
    new_state_lru = jnp.stack(lru_states, axis=1).astype(x_prompt.dtype)
    new_state_ssd = jnp.stack(ssd_states, axis=1).astype(x_prompt.dtype)
    return (y_prompt, y_sample, new_state_lru, new_state_ssd)
```

```python
import functools
import math

import numpy as np
import jax
import jax.numpy as jnp
from jax import lax
from jax.experimental import pallas as pl
from jax.experimental.pallas import tpu as pltpu

F32 = jnp.float32
BF16 = jnp.bfloat16

EPS = 1e-6
GRID_W = 64
CONV_W = 4
CONV_LEFT = 1
LRU_C = 8.0
POOL_WINDOWS = (2, 4, 8, 16)
N_EXPERT_GROUPS = 4
TOP_K = 2

LANES = 128
HALO = 16
VMEM_LIMIT = 56 * 1024 * 1024


def _cparams(sem):
    return pltpu.CompilerParams(dimension_semantics=sem, vmem_limit_bytes=VMEM_LIMIT)


def _sigmoid(x):
    return 1.0 / (1.0 + jnp.exp(-x))


def _silu(x):
    return x * _sigmoid(x)


def _softplus(x):
    return jnp.maximum(x, 0.0) + jnp.log(1.0 + jnp.exp(-jnp.abs(x)))


def _gelu_tanh(x):
    return x * (0.5 * (1.0 + jnp.tanh(math.sqrt(2.0 / math.pi) * (x + 0.044715 * (x * x * x)))))


class Dims:
    def __init__(self, x_prompt, x_sample, state_ssd, w_r, w_pool, w_gate, dt_bias, conv_b_w, w_oa, w_ob, w_oc):
        self.Bp, self.Lp, self.D = x_prompt.shape
        self.Bs, self.Ls, _ = x_sample.shape
        self.depth = w_r.shape[0]
        self.Ts = self.Bs * self.Ls
        self.Tp = self.Bp * self.Lp
        self.T = self.Ts + self.Tp
        self.nseq = self.Bs + self.Bp
        self.d_lru = w_oa.shape[1]
        self.lru_blocks = w_r.shape[2]
        self.lru_bw = w_r.shape[3]
        self.d_ssd = w_ob.shape[1]
        self.H = dt_bias.shape[2]
        self.P = self.d_ssd // self.H
        self.N = state_ssd.shape[-1]
        self.d_xbc = conv_b_w.shape[2]
        self.G = (self.d_xbc - self.d_ssd) // (2 * self.N)
        self.R = self.H // self.G
        self.d_pool = w_oc.shape[1]
        self.pool_group = w_pool.shape[2]
        self.E = w_gate.shape[1]
        self.F = w_gate.shape[3]
        self.o_xbc = 0
        self.o_xa = self.d_xbc
        self.o_ga = self.o_xa + self.d_lru
        self.o_xc = self.o_ga + self.d_lru
        self.o_z = self.o_xc + self.d_pool
        self.o_gate = self.o_z + self.d_ssd
        self.n_main = self.o_gate + 3 * self.D


def _seq_tile_info(dm, tile, tl):
    row0 = tile * tl
    in_sample = row0 < dm.Ts
    rp = jnp.maximum(row0 - dm.Ts, 0)
    pos = jnp.where(in_sample, row0 % dm.Ls, rp % dm.Lp)
    lseq = jnp.where(in_sample, dm.Ls, dm.Lp)
    seq = jnp.where(in_sample, row0 // dm.Ls, dm.Bs + rp // dm.Lp)
    return pos, lseq, seq


def _cond_idx(dm, i, tm):
    return jnp.minimum((i * tm) // dm.Ls, dm.Bs)


def _mod_kernel(c_ref, w_ref, b_ref, o_ref):
    c = c_ref[...]
    a = _silu(c).astype(BF16)
    o_ref[...] = jnp.dot(a, w_ref[...].astype(BF16), preferred_element_type=F32) + b_ref[...]


def _mod_table(cond, w_mod, b_mod, tn=512):
    depth, d, n6 = w_mod.shape
    nc = cond.shape[0]
    return pl.pallas_call(
        _mod_kernel,
        grid=(depth, n6 // tn),
        in_specs=[
            pl.BlockSpec((nc, d), lambda l, j: (0, 0)),
            pl.BlockSpec((None, d, tn), lambda l, j: (l, 0, j)),
            pl.BlockSpec((None, 1, tn), lambda l, j: (l, 0, j)),
        ],
        out_specs=pl.BlockSpec((None, nc, tn), lambda l, j: (l, 0, j)),
        out_shape=jax.ShapeDtypeStruct((depth, nc, n6), F32),
        compiler_params=_cparams(("arbitrary", "arbitrary")),
        name="mod_table",
    )(cond, w_mod, b_mod.reshape(depth, 1, n6))


def _norm_mod(x, g, sc, sh):
    ms = jnp.mean(x * x, axis=-1, keepdims=True)
    y = x * lax.rsqrt(ms + EPS) * g
    return y * (1.0 + sc) + sh


def _mod_spec(dm, layer, section, tm):
    return pl.BlockSpec((None, None, 1, dm.D), lambda i: (layer, _cond_idx(dm, i, tm), 0, section))


def _embed_norm_kernel(dm, tm, xs_ref, xp_ref, pos_ref, g_ref, sc_ref, sh_ref, x_ref, u_ref):
    i = pl.program_id(0)
    n_s = dm.Ts // tm

    @pl.when(i < n_s)
    def _():
        x_ref[...] = xs_ref[...] + pos_ref[...]

    @pl.when(i >= n_s)
    def _():
        x_ref[...] = xp_ref[...]

    u_ref[...] = _norm_mod(x_ref[...], g_ref[...], sc_ref[...], sh_ref[...]).astype(u_ref.dtype)


def _embed_norm(dm, xs2, xp2, pos, g, mod4, tm=256):
    n_s = dm.Ts // tm
    n_l = dm.Ls // tm
    row = pl.BlockSpec((tm, dm.D), lambda i: (i, 0))
    return pl.pallas_call(
        functools.partial(_embed_norm_kernel, dm, tm),
        grid=(dm.T // tm,),
        in_specs=[
            pl.BlockSpec((tm, dm.D), lambda i: (jnp.minimum(i, n_s - 1), 0)),
            pl.BlockSpec((tm, dm.D), lambda i: (jnp.maximum(i - n_s, 0), 0)),
            pl.BlockSpec((tm, dm.D), lambda i: (i % n_l, 0)),
            pl.BlockSpec((1, dm.D), lambda i: (0, 0)),
            _mod_spec(dm, 0, 1, tm),
            _mod_spec(dm, 0, 0, tm),
        ],
        out_specs=[row, row],
        out_shape=[jax.ShapeDtypeStruct((dm.T, dm.D), F32), jax.ShapeDtypeStruct((dm.T, dm.D), BF16)],
        compiler_params=_cparams(("arbitrary",)),
        name="embed_norm",
    )(xs2, xp2, pos, g, mod4, mod4)


def _route_rows(s, sel):
    e_total = s.shape[0]
    per = e_total // N_EXPERT_GROUPS
    assert per == 4 and TOP_K == 2
    srow = [s[e:e + 1, :] for e in range(e_total)]
    vrow = [sel[e:e + 1, :] for e in range(e_total)]
    scores = []
    for gi in range(N_EXPERT_GROUPS):
        a, b, c, d = vrow[4 * gi:4 * gi + 4]
        hi1, lo1 = jnp.maximum(a, b), jnp.minimum(a, b)
        hi2, lo2 = jnp.maximum(c, d), jnp.minimum(c, d)
        top = jnp.maximum(hi1, hi2)
        second = jnp.maximum(jnp.minimum(hi1, hi2), jnp.maximum(lo1, lo2))
        scores.append(top + second)
    g = jnp.zeros_like(scores[0])
    best = scores[0]
    for gi in range(1, N_EXPERT_GROUPS):
        upd = scores[gi] > best
        g = jnp.where(upd, float(gi), g)
        best = jnp.where(upd, scores[gi], best)

    def pick(rows, j):
        out = rows[j]
        for gi in range(1, N_EXPERT_GROUPS):
            out = jnp.where(g == float(gi), rows[4 * gi + j], out)
        return out

    v = [pick(vrow, j) for j in range(4)]
    sv = [pick(srow, j) for j in range(4)]
    i1 = jnp.zeros_like(g)
    b1 = v[0]
    for j in range(1, 4):
        upd = v[j] > b1
        i1 = jnp.where(upd, float(j), i1)
        b1 = jnp.where(upd, v[j], b1)
    i2 = jnp.zeros_like(g)
    b2 = jnp.full_like(g, -jnp.inf)
    for j in range(4):
        upd = jnp.logical_and(i1 != float(j), v[j] > b2)
        i2 = jnp.where(upd, float(j), i2)
        b2 = jnp.where(upd, v[j], b2)

    def pick_idx(idx):
        out = sv[0]
        for j in range(1, 4):
            out = jnp.where(idx == float(j), sv[j], out)
        return out

    s1, s2 = pick_idx(i1), pick_idx(i2)
    tot = s1 + s2
    return g * 4.0 + i1, g * 4.0 + i2, s1 / tot, s2 / tot


def _norm_router_kernel(x_ref, g_ref, sc_ref, sh_ref, wr_ref, br_ref, u_ref, r_ref):
    u = _norm_mod(x_ref[...], g_ref[...], sc_ref[...], sh_ref[...])
    u_ref[...] = u.astype(u_ref.dtype)
    logits = jnp.dot(u, wr_ref[...], preferred_element_type=F32, precision=lax.Precision.HIGHEST)
    e_total = br_ref.shape[0]
    lt = logits.T[:e_total, :]
    s = _sigmoid(lt)
    sel = s + br_ref[...]
    e1, e2, w1, w2 = _route_rows(s, sel)
    rows = lax.broadcasted_iota(jnp.int32, r_ref.shape, 0)
    r_ref[...] = jnp.where(rows == 0, e1, jnp.where(rows == 1, e2, jnp.where(rows == 2, w1,
                                                                             jnp.where(rows == 3, w2, 0.0))))


def _norm_router(dm, layer, x, g, mod4, wr_pad, br_col, tm=256):
    row = pl.BlockSpec((tm, dm.D), lambda i: (i, 0))
    return pl.pallas_call(
        _norm_router_kernel,
        grid=(dm.T // tm,),
        in_specs=[
            row,
            pl.BlockSpec((1, dm.D), lambda i: (0, 0)),
            _mod_spec(dm, layer, 4, tm),
            _mod_spec(dm, layer, 3, tm),
            pl.BlockSpec(wr_pad.shape, lambda i: (0, 0)),
            pl.BlockSpec(br_col.shape, lambda i: (0, 0)),
        ],
        out_specs=[row, pl.BlockSpec((8, tm), lambda i: (0, i))],
        out_shape=[jax.ShapeDtypeStruct((dm.T, dm.D), BF16), jax.ShapeDtypeStruct((8, dm.T), F32)],
        compiler_params=_cparams(("arbitrary",)),
        name="norm_router",
    )(x, g, mod4, mod4, wr_pad, br_col)


def _combine_norm_kernel(x_ref, y1_ref, y2_ref, g2_ref, g_ref, sc_ref, sh_ref, xo_ref, u_ref):
    x = x_ref[...] + g2_ref[...] * (y1_ref[...].astype(F32) + y2_ref[...].astype(F32))
    xo_ref[...] = x
    u_ref[...] = _norm_mod(x, g_ref[...], sc_ref[...], sh_ref[...]).astype(u_ref.dtype)


def _combine_norm(dm, layer, x, y1, y2, g_next, mod4, tm=256):
    row = pl.BlockSpec((tm, dm.D), lambda i: (i, 0))
    return pl.pallas_call(
        _combine_norm_kernel,
        grid=(dm.T // tm,),
        in_specs=[row, row, row, _mod_spec(dm, layer, 5, tm), pl.BlockSpec((1, dm.D), lambda i: (0, 0)),
                  _mod_spec(dm, layer + 1, 1, tm), _mod_spec(dm, layer + 1, 0, tm)],
        out_specs=[row, row],
        out_shape=[jax.ShapeDtypeStruct((dm.T, dm.D), F32), jax.ShapeDtypeStruct((dm.T, dm.D), BF16)],
        compiler_params=_cparams(("arbitrary",)),
        name="combine_norm",
    )(x, y1, y2, mod4, g_next, mod4, mod4)


def _combine_final_kernel(x_ref, y1_ref, y2_ref, g2_ref, g_ref, o_ref):
    x = x_ref[...] + g2_ref[...] * (y1_ref[...].astype(F32) + y2_ref[...].astype(F32))
    ms = jnp.mean(x * x, axis=-1, keepdims=True)
    o_ref[...] = x * lax.rsqrt(ms + EPS) * g_ref[...]


def _combine_final(dm, layer, x, y1, y2, g_final, mod4, row_start, n_rows, tm=256):
    off = row_start // tm
    row = pl.BlockSpec((tm, dm.D), lambda i: (i + off, 0))
    return pl.pallas_call(
        _combine_final_kernel,
        grid=(n_rows // tm,),
        in_specs=[row, row, row,
                  pl.BlockSpec((None, None, 1, dm.D), lambda i: (layer, _cond_idx(dm, i + off, tm), 0, 5)),
                  pl.BlockSpec((1, dm.D), lambda i: (0, 0))],
        out_specs=pl.BlockSpec((tm, dm.D), lambda i: (i, 0)),
        out_shape=jax.ShapeDtypeStruct((n_rows, dm.D), F32),
        compiler_params=_cparams(("arbitrary",)),
        name="combine_final",
    )(x, y1, y2, mod4, g_final)


def _mm_kernel(a_ref, w_ref, o_ref):
    o_ref[...] = jnp.dot(a_ref[...], w_ref[...], preferred_element_type=F32).astype(o_ref.dtype)


def _matmul(a, w, out_dtype, tm, tn, name):
    m, k = a.shape
    n = w.shape[1]
    return pl.pallas_call(
        _mm_kernel,
        grid=(n // tn, m // tm),
        in_specs=[pl.BlockSpec((tm, k), lambda j, i: (i, 0)), pl.BlockSpec((k, tn), lambda j, i: (0, j))],
        out_specs=pl.BlockSpec((tm, tn), lambda j, i: (i, j)),
        out_shape=jax.ShapeDtypeStruct((m, n), out_dtype),
        compiler_params=_cparams(("arbitrary", "arbitrary")),
        name=name,
    )(a, w)


def _merge_kernel(ya_ref, yb_ref, yc_ref, g0_ref, g1_ref, g2_ref, wa_ref, wb_ref, wc_ref, o_ref):
    acc = _sigmoid(g0_ref[...].astype(F32)) * jnp.dot(ya_ref[...], wa_ref[...], preferred_element_type=F32)
    acc += _sigmoid(g1_ref[...].astype(F32)) * jnp.dot(yb_ref[...], wb_ref[...], preferred_element_type=F32)
    acc += _sigmoid(g2_ref[...].astype(F32)) * jnp.dot(yc_ref[...], wc_ref[...], preferred_element_type=F32)
    o_ref[...] = acc.astype(o_ref.dtype)


def _merge(dm, ya, yb, yc, proj, wa, wb, wc, tm=512, tn=1024):
    tn = min(tn, dm.D)
    nb = dm.D // tn
    gb = dm.o_gate // tn

    def gate_spec(k):
        return pl.BlockSpec((tm, tn), lambda j, i: (i, gb + k * nb + j))

    def a_spec(kd):
        return pl.BlockSpec((tm, kd), lambda j, i: (i, 0))

    def w_spec(kd):
        return pl.BlockSpec((kd, tn), lambda j, i: (0, j))

    return pl.pallas_call(
        _merge_kernel,
        grid=(nb, dm.T // tm),
        in_specs=[a_spec(dm.d_lru), a_spec(dm.d_ssd), a_spec(dm.d_pool), gate_spec(0), gate_spec(1), gate_spec(2),
                  w_spec(dm.d_lru), w_spec(dm.d_ssd), w_spec(dm.d_pool)],
        out_specs=pl.BlockSpec((tm, tn), lambda j, i: (i, j)),
        out_shape=jax.ShapeDtypeStruct((dm.T, dm.D), BF16),
        compiler_params=_cparams(("arbitrary", "arbitrary")),
        name="merge",
    )(ya, yb, yc, proj, proj, proj, wa, wb, wc)


def _outproj_kernel(a_ref, w_ref, x_ref, g_ref, o_ref):
    o_ref[...] = x_ref[...] + g_ref[...] * jnp.dot(a_ref[...], w_ref[...], preferred_element_type=F32)


def _outproj(dm, layer, merged, w_out, x, mod4, tm=512, tn=1024):
    tn = min(tn, dm.D)
    nb = dm.D // tn
    return pl.pallas_call(
        _outproj_kernel,
        grid=(nb, dm.T // tm),
        in_specs=[
            pl.BlockSpec((tm, dm.D), lambda j, i: (i, 0)),
            pl.BlockSpec((dm.D, tn), lambda j, i: (0, j)),
            pl.BlockSpec((tm, tn), lambda j, i: (i, j)),
            pl.BlockSpec((None, None, 1, tn), lambda j, i: (layer, _cond_idx(dm, i, tm), 0, 2 * nb + j)),
        ],
        out_specs=pl.BlockSpec((tm, tn), lambda j, i: (i, j)),
        out_shape=jax.ShapeDtypeStruct((dm.T, dm.D), F32),
        compiler_params=_cparams(("arbitrary", "arbitrary")),
        name="outproj",
    )(merged, w_out, x, mod4)


def _halo_specs(dm, tl, width, col_block, tile_of):
    per = tl // HALO
    last = dm.T // HALO - 1
    cur = pl.BlockSpec((tl, width), lambda i: (tile_of(i), col_block))
    prev = pl.BlockSpec((HALO, width), lambda i: (jnp.maximum(tile_of(i) * per - 1, 0), col_block))
    nxt = pl.BlockSpec((HALO, width), lambda i: (jnp.minimum((tile_of(i) + 1) * per, last), col_block))
    return [cur, prev, nxt]


def _fill_ext(ext_ref, cur_ref, prev_ref, next_ref, is_first, is_last, tl):
    ext_ref[pl.ds(HALO, tl), :] = cur_ref[...].astype(F32)
    ext_ref[pl.ds(0, HALO), :] = jnp.where(is_first, 0.0, prev_ref[...].astype(F32))
    ext_ref[pl.ds(HALO + tl, HALO), :] = jnp.where(is_last, 0.0, next_ref[...].astype(F32))


def _conv_from_ext(ext_ref, w_ref, b_ref, tl):
    y = b_ref[...] + ext_ref[pl.ds(HALO - CONV_LEFT, tl), :] * w_ref[0:1, :]
    for k in range(1, CONV_W):
        y = y + ext_ref[pl.ds(HALO - CONV_LEFT + k, tl), :] * w_ref[k:k + 1, :]
    return y


def _lru_kernel(dm, tl, reverse, *refs):
    if reverse:
        (xa_ref, xp_ref, xn_ref, cw_ref, cb_ref, wr_ref, br_ref, wi_ref, bi_ref, lam_ref, h0_ref,
         ga_ref, hf_ref, y_ref, hT_ref, ext_ref, a_ref, bx_ref, hs_ref, h_ref) = refs
    else:
        (xa_ref, xp_ref, xn_ref, cw_ref, cb_ref, wr_ref, br_ref, wi_ref, bi_ref, lam_ref, h0_ref,
         y_ref, hT_ref, ext_ref, a_ref, bx_ref, hs_ref, h_ref) = refs
    nt = dm.T // tl
    i = pl.program_id(0)
    tile = nt - 1 - i if reverse else i
    pos, lseq, _ = _seq_tile_info(dm, tile, tl)
    is_first = pos == 0
    is_last = pos + tl == lseq
    _fill_ext(ext_ref, xa_ref, xp_ref, xn_ref, is_first, is_last, tl)
    xc = _conv_from_ext(ext_ref, cw_ref, cb_ref, tl)
    sp = _softplus(-lam_ref[...])
    bw = dm.lru_bw
    for n in range(dm.lru_blocks):
        cs = slice(n * bw, (n + 1) * bw)
        xb = xc[:, cs]
        xbh = xb.astype(BF16)
        r = _sigmoid(jnp.dot(xbh, wr_ref[n], preferred_element_type=F32) + br_ref[:, cs])
        ig = _sigmoid(jnp.dot(xbh, wi_ref[n], preferred_element_type=F32) + bi_ref[:, cs])
        log_a = (-LRU_C) * r * sp[:, cs]
        a_ref[:, cs] = jnp.exp(log_a)
        bx_ref[:, cs] = jnp.sqrt(1.0 - jnp.exp(2.0 * log_a)) * (ig * xb)

    start_of_scan = is_last if reverse else is_first

    @pl.when(start_of_scan)
    def _():
        h_ref[...] = h0_ref[...]

    def body(t, h):
        tt = tl - 1 - t if reverse else t
        h = a_ref[pl.ds(tt, 1), :] * h + bx_ref[pl.ds(tt, 1), :]
        hs_ref[pl.ds(tt, 1), :] = h
        return h

    h = lax.fori_loop(0, tl, body, h_ref[...], unroll=8)
    h_ref[...] = h
    hT_ref[...] = h
    if reverse:
        y = (hf_ref[...].astype(F32) + hs_ref[...]) * _gelu_tanh(ga_ref[...].astype(F32))
        y_ref[...] = y.astype(y_ref.dtype)
    else:
        y_ref[...] = hs_ref[...].astype(y_ref.dtype)


def _lru(dm, reverse, proj, cw, cb, wr, br, wi, bi, lam, h0, hs_fwd=None, tl=256):
    nt = dm.T // tl
    c = dm.d_lru
    tile_of = (lambda i: nt - 1 - i) if reverse else (lambda i: i)
    seq_of = lambda i: _seq_tile_info(dm, tile_of(i), tl)[2]
    full = lambda shape: pl.BlockSpec(shape, lambda i: (0,) * len(shape))
    row = pl.BlockSpec((tl, c), lambda i: (tile_of(i), 0))
    in_specs = _halo_specs(dm, tl, c, dm.o_xa // c, tile_of) + [
        full((CONV_W, c)), full((1, c)),
        full(wr.shape), full((1, c)), full(wi.shape), full((1, c)), full((1, c)),
        pl.BlockSpec((None, 1, c), lambda i: (seq_of(i), 0, 0)),
    ]
    args = [proj, proj, proj, cw, cb, wr, br, wi, bi, lam, h0]
    if reverse:
        in_specs += [pl.BlockSpec((tl, c), lambda i: (tile_of(i), dm.o_ga // c)), row]
        args += [proj, hs_fwd]
    return pl.pallas_call(
        functools.partial(_lru_kernel, dm, tl, reverse),
        grid=(nt,),
        in_specs=in_specs,
        out_specs=[row, pl.BlockSpec((None, 1, c), lambda i: (seq_of(i), 0, 0))],
        out_shape=[jax.ShapeDtypeStruct((dm.T, c), BF16), jax.ShapeDtypeStruct((dm.nseq, 1, c), F32)],
        scratch_shapes=[pltpu.VMEM((tl + 2 * HALO, c), F32), pltpu.VMEM((tl, c), F32), pltpu.VMEM((tl, c), F32),
                        pltpu.VMEM((tl, c), F32), pltpu.VMEM((1, c), F32)],
        compiler_params=_cparams(("arbitrary",)),
        name="lru_bwd" if reverse else "lru_fwd",
    )(*args)


def _pool_kernel(dm, tl, xc_ref, xp_ref, xn_ref, w_ref, s_ref, o_ref, ext_ref):
    tile = pl.program_id(0)
    pos, lseq, _ = _seq_tile_info(dm, tile, tl)
    _fill_ext(ext_ref, xc_ref, xp_ref, xn_ref, pos == 0, pos + tl == lseq, tl)
    t = pos + lax.broadcasted_iota(jnp.int32, (tl, 1), 0)
    pg = dm.pool_group
    for k, w in enumerate(POOL_WINDOWS):
        cs = slice(k * pg, (k + 1) * pg)
        acc = ext_ref[pl.ds(HALO - w // 2, tl), cs]
        for o in range(-w // 2 + 1, w // 2):
            acc = acc + ext_ref[pl.ds(HALO + o, tl), cs]
        cnt = (jnp.minimum(t + w // 2, lseq) - jnp.maximum(t - w // 2, 0)).astype(F32)
        dev = acc / cnt - ext_ref[pl.ds(HALO, tl), cs]
        y = jnp.dot(dev.astype(BF16), w_ref[k], preferred_element_type=F32)
        o_ref[:, cs] = (y * s_ref[:, cs]).astype(o_ref.dtype)


def _pool(dm, proj, w_pool, scale, tl=256):
    c = dm.d_pool
    return pl.pallas_call(
        functools.partial(_pool_kernel, dm, tl),
        grid=(dm.T // tl,),
        in_specs=_halo_specs(dm, tl, c, dm.o_xc // c, lambda i: i) + [
            pl.BlockSpec(w_pool.shape, lambda i: (0, 0, 0)), pl.BlockSpec((1, c), lambda i: (0, 0))],
        out_specs=pl.BlockSpec((tl, c), lambda i: (i, 0)),
        out_shape=jax.ShapeDtypeStruct((dm.T, c), BF16),
        scratch_shapes=[pltpu.VMEM((tl + 2 * HALO, c), F32)],
        compiler_params=_cparams(("arbitrary",)),
        name="pool",
    )(proj, proj, proj, w_pool, scale)


def _dot_exact(a, b):
    return jnp.dot(a, b, preferred_element_type=F32, precision=lax.Precision.HIGHEST)


def _ssd_kernel(dm, q, reverse, *refs):
    if reverse:
        (x_ref, xp_ref, xn_ref, dt_ref, cw_ref, cb_ref, dtb_ref, alog_ref, h0_ref,
         z_ref, yf_ref, ng_ref, y_ref, hT_ref, ext_ref, h_ref, yacc_ref) = refs
    else:
        (x_ref, xp_ref, xn_ref, dt_ref, cw_ref, cb_ref, dtb_ref, alog_ref, h0_ref,
         dsk_ref, y_ref, hT_ref, ext_ref, h_ref, yacc_ref) = refs
    nt = dm.T // q
    i = pl.program_id(0)
    tile = nt - 1 - i if reverse else i
    pos, lseq, _ = _seq_tile_info(dm, tile, q)
    is_first = pos == 0
    is_last = pos + q == lseq
    _fill_ext(ext_ref, x_ref, xp_ref, xn_ref, is_first, is_last, q)
    xbc = _silu(_conv_from_ext(ext_ref, cw_ref, cb_ref, q))
    d_ssd, n, g_cnt, r_cnt, p = dm.d_ssd, dm.N, dm.G, dm.R, dm.P
    gn = g_cnt * n
    pair = 2 * p
    assert pair == LANES and n == LANES and q == LANES

    @pl.when(is_last if reverse else is_first)
    def _():
        h_ref[...] = h0_ref[...]

    dt = _softplus(dt_ref[...] + dtb_ref[...])
    a_neg = -jnp.exp(alog_ref[...])
    dta = dt * a_neg
    ri = lax.broadcasted_iota(jnp.int32, (q, q), 0)
    ci = lax.broadcasted_iota(jnp.int32, (q, q), 1)
    valid = (ci >= ri) if reverse else (ci <= ri)
    a_cs = _dot_exact(valid.astype(F32), dta)
    a_end = a_cs[0:1, :] if reverse else a_cs[q - 1:q, :]
    w_state = dt * jnp.exp(a_end - a_cs)
    a_cs_t = a_cs.T
    dt_t = dt.T
    cd_t = jnp.broadcast_to(jnp.exp(a_end), (q, LANES)).T
    lane = lax.broadcasted_iota(jnp.int32, (q, LANES), 1)
    low_half = lane < p

    for g in range(g_cnt):
        b_g = xbc[:, d_ssd + g * n: d_ssd + (g + 1) * n]
        c_g = xbc[:, d_ssd + gn + g * n: d_ssd + gn + (g + 1) * n]
        b_h = b_g.astype(BF16)
        scores = lax.dot_general(c_g.astype(BF16), b_h, (((1,), (1,)), ((), ())), preferred_element_type=F32)
        for k in range(r_cnt // 2):
            c0 = (g * r_cnt + 2 * k) * p
            xs_pair = xbc[:, c0:c0 + pair]
            xs_h = xs_pair.astype(BF16)
            h_pair = h_ref[pl.ds(c0, pair), :]
            h_h = h_pair.astype(BF16)
            ys = []
            cols = []
            for s in range(2):
                hd = g * r_cnt + 2 * k + s
                col = jnp.broadcast_to(a_cs[:, hd:hd + 1], (q, q))
                row = jnp.broadcast_to(a_cs_t[hd:hd + 1, :], (q, q))
                decay = jnp.exp(jnp.where(valid, col - row, -jnp.inf))
                m = (scores * decay * dt_t[hd:hd + 1, :]).astype(BF16)
                y_d = jnp.dot(m, xs_h, preferred_element_type=F32)
                c_s = (c_g * jnp.exp(col)).astype(BF16)
                y_o = lax.dot_general(c_s, h_h, (((1,), (1,)), ((), ())), preferred_element_type=F32)
                ys.append(y_d + y_o)
                cols.append(jnp.broadcast_to(w_state[:, hd:hd + 1], (q, LANES)))
            y_pair = jnp.where(low_half, ys[0], ys[1])
            xsw = (xs_pair * jnp.where(low_half, cols[0], cols[1])).astype(BF16)
            st = lax.dot_general(xsw, b_h, (((0,), (0,)), ((), ())), preferred_element_type=F32)
            hd0 = g * r_cnt + 2 * k
            dec = jnp.concatenate([jnp.broadcast_to(cd_t[hd0:hd0 + 1, :], (p, n)),
                                   jnp.broadcast_to(cd_t[hd0 + 1:hd0 + 2, :], (p, n))], axis=0)
            h_ref[pl.ds(c0, pair), :] = dec * h_pair + st
            if reverse:
                yacc_ref[:, c0:c0 + pair] = y_pair
            else:
                y_ref[:, c0:c0 + pair] = (y_pair + dsk_ref[:, c0:c0 + pair] * xs_pair).astype(y_ref.dtype)

    hT_ref[...] = h_ref[...]
    if reverse:
        gw = d_ssd // g_cnt
        for g in range(g_cnt):
            cs = slice(g * gw, (g + 1) * gw)
            y = (yacc_ref[:, cs] + yf_ref[:, cs]) * _silu(z_ref[:, cs].astype(F32))
            ms = jnp.mean(y * y, axis=-1, keepdims=True)
            y_ref[:, cs] = (y * lax.rsqrt(ms + EPS) * ng_ref[:, cs]).astype(y_ref.dtype)


def _ssd(dm, reverse, proj, dtp, cw, cb, dtb, alog, h0, extra, y_fwd=None, q=128):
    nt = dm.T // q
    tile_of = (lambda i: nt - 1 - i) if reverse else (lambda i: i)
    seq_of = lambda i: _seq_tile_info(dm, tile_of(i), q)[2]
    full = lambda shape: pl.BlockSpec(shape, lambda i: (0,) * len(shape))
    c = dm.d_xbc
    hp = dm.H * dm.P
    rowy = pl.BlockSpec((q, dm.d_ssd), lambda i: (tile_of(i), 0))
    state = pl.BlockSpec((None, hp, dm.N), lambda i: (seq_of(i), 0, 0))
    in_specs = _halo_specs(dm, q, c, 0, tile_of) + [
        pl.BlockSpec((q, LANES), lambda i: (tile_of(i), 1 if reverse else 0)),
        full((CONV_W, c)), full((1, c)), full((1, LANES)), full((1, LANES)), state]
    args = [proj, proj, proj, dtp, cw, cb, dtb, alog, h0]
    if reverse:
        in_specs += [pl.BlockSpec((q, dm.d_ssd), lambda i: (tile_of(i), dm.o_z // dm.d_ssd)), rowy,
                     full((1, dm.d_ssd))]
        args += [proj, y_fwd, extra]
    else:
        in_specs += [full((1, dm.d_ssd))]
        args += [extra]
    return pl.pallas_call(
        functools.partial(_ssd_kernel, dm, q, reverse),
        grid=(nt,),
        in_specs=in_specs,
        out_specs=[rowy, state],
        out_shape=[jax.ShapeDtypeStruct((dm.T, dm.d_ssd), BF16 if reverse else F32),
                   jax.ShapeDtypeStruct((dm.nseq, hp, dm.N), F32)],
        scratch_shapes=[pltpu.VMEM((q + 2 * HALO, c), F32), pltpu.VMEM((hp, dm.N), F32),
                        pltpu.VMEM((q, dm.d_ssd), F32)],
        compiler_params=_cparams(("arbitrary",)),
        name="ssd_bwd" if reverse else "ssd_fwd",
    )(*args)


def _expert_up_kernel(te_ref, nu_ref, x_ref, wg_ref, wu_ref, h_ref):
    @pl.when(pl.program_id(0) < nu_ref[0])
    def _():
        x = x_ref[...]
        a = jnp.dot(x, wg_ref[...], preferred_element_type=F32)
        b = jnp.dot(x, wu_ref[...], preferred_element_type=F32)
        h_ref[...] = (_silu(a) * b).astype(h_ref.dtype)


def _expert_down_kernel(te_ref, nu_ref, h_ref, wd_ref, w_ref, y_ref):
    @pl.when(pl.program_id(0) < nu_ref[0])
    def _():
        y = jnp.dot(h_ref[...], wd_ref[...], preferred_element_type=F32)
        y_ref[...] = (y * w_ref[...]).astype(y_ref.dtype)


def _experts(dm, xs, wg, wu, wd, wgt, tile_expert, n_used, tm):
    p_pad = xs.shape[0]
    n_tiles = p_pad // tm
    d, f = dm.D, dm.F
    h = pl.pallas_call(
        _expert_up_kernel,
        grid_spec=pltpu.PrefetchScalarGridSpec(
            num_scalar_prefetch=2, grid=(n_tiles,),
            in_specs=[pl.BlockSpec((tm, d), lambda i, te, nu: (i, 0)),
                      pl.BlockSpec((None, d, f), lambda i, te, nu: (te[i], 0, 0)),
                      pl.BlockSpec((None, d, f), lambda i, te, nu: (te[i], 0, 0))],
            out_specs=pl.BlockSpec((tm, f), lambda i, te, nu: (i, 0))),
        out_shape=jax.ShapeDtypeStruct((p_pad, f), BF16),
        compiler_params=_cparams(("arbitrary",)),
        name="expert_up",
    )(tile_expert, n_used, xs, wg, wu)
    return pl.pallas_call(
        _expert_down_kernel,
        grid_spec=pltpu.PrefetchScalarGridSpec(
            num_scalar_prefetch=2, grid=(n_tiles,),
            in_specs=[pl.BlockSpec((tm, f), lambda i, te, nu: (i, 0)),
                      pl.BlockSpec((None, f, d), lambda i, te, nu: (te[i], 0, 0)),
                      pl.BlockSpec((tm, 1), lambda i, te, nu: (i, 0))],
            out_specs=pl.BlockSpec((tm, d), lambda i, te, nu: (i, 0))),
        out_shape=jax.ShapeDtypeStruct((p_pad, d), BF16),
        compiler_params=_cparams(("arbitrary",)),
        name="expert_down",
    )(tile_expert, n_used, h, wd, wgt)


def _dispatch_plan(dm, route, tm):
    t = dm.T
    e_cnt = dm.E
    e_idx = route[0:2].astype(jnp.int32).reshape(-1)
    w = route[2:4].reshape(-1)
    onehot = (e_idx[:, None] == jnp.arange(e_cnt, dtype=jnp.int32)[None, :]).astype(jnp.int32)
    rank = jnp.sum((jnp.cumsum(onehot, axis=0) - onehot) * onehot, axis=1)
    counts = jnp.sum(onehot, axis=0)
    tiles_per = (counts + tm - 1) // tm
    tile_end = jnp.cumsum(tiles_per)
    start = (tile_end - tiles_per) * tm
    slot = start[e_idx] + rank
    n_tiles = (TOP_K * t) // tm + e_cnt
    p_pad = n_tiles * tm
    tok = jnp.zeros((p_pad,), jnp.int32).at[slot].set(jnp.tile(jnp.arange(t, dtype=jnp.int32), TOP_K))
    wgt = jnp.zeros((p_pad,), F32).at[slot].set(w)
    tile_expert = jnp.minimum(jnp.searchsorted(tile_end, jnp.arange(n_tiles, dtype=jnp.int32), side="right"),
                              e_cnt - 1).astype(jnp.int32)
    n_used = tile_end[-1:].astype(jnp.int32)
    return tok, wgt.reshape(p_pad, 1), tile_expert, n_used, slot[:t], slot[t:]


def _grid_pos_embed(n_tokens, d_model):
    rows = n_tokens // GRID_W
    row = jnp.repeat(jnp.arange(rows), GRID_W).astype(F32)
    col = jnp.tile(jnp.arange(GRID_W), rows).astype(F32)
    quarter = d_model // 4
    omega = 1.0 / (10000.0 ** (jnp.arange(quarter, dtype=F32) / quarter))
    ang_r = row[:, None] * omega[None, :]
    ang_c = col[:, None] * omega[None, :]
    return jnp.concatenate([jnp.sin(ang_r), jnp.cos(ang_r), jnp.sin(ang_c), jnp.cos(ang_c)], axis=-1)


def _pad_lanes(v, fill=0.0):
    return jnp.pad(v.astype(F32), (0, LANES - v.shape[0]), constant_values=fill).reshape(1, LANES)


def kernel(x_prompt, x_sample, state_lru, state_ssd, c, c_ctx, w_mod, b_mod, norm1_g, w_in, conv_a_w, conv_a_b, w_r, b_r, w_i, b_i, lru_lambda, w_oa, conv_b_w, conv_b_b, dt_bias, a_log, d_skip, ssd_norm_g, w_ob, w_pool, pool_scale, w_oc, w_out, norm2_g, w_router, b_router, w_gate, w_up, w_down, final_g):
    dm = Dims(x_prompt, x_sample, state_ssd, w_r, w_pool, w_gate, dt_bias, conv_b_w, w_oa, w_ob, w_oc)
    d = dm.D
    moe_tm = min(512, dm.Lp)

    ncond = 16
    cond = jnp.zeros((ncond, d), F32).at[:dm.Bs].set(c).at[dm.Bs].set(c_ctx)
    mod4 = _mod_table(cond, w_mod, b_mod).reshape(dm.depth, ncond, 1, 6 * d)

    pos = _grid_pos_embed(dm.Ls, d)
    x, u = _embed_norm(dm, x_sample.reshape(dm.Ts, d), x_prompt.reshape(dm.Tp, d), pos,
                       norm1_g[0].reshape(1, d), mod4)

    wr_pad = jnp.pad(w_router, ((0, 0), (0, LANES - dm.E)))
    br_col = b_router.reshape(dm.E, 1).astype(F32)
    hp = dm.H * dm.P
    lru_states, ssd_states = [], []
    out = None
    for l in range(dm.depth):
        wl = w_in[l]
        o = 0
        parts = {}
        for name, size in (("xa", dm.d_lru), ("ga", dm.d_lru), ("z", dm.d_ssd), ("xbc", dm.d_xbc),
                           ("dt", 2 * dm.H), ("xc", dm.d_pool), ("gate", 3 * d)):
            parts[name] = wl[:, o:o + size]
            o += size
        w_main = jnp.concatenate([parts[k] for k in ("xbc", "xa", "ga", "xc", "z", "gate")], axis=1).astype(BF16)
        w_dt = jnp.concatenate([jnp.pad(parts["dt"][:, :dm.H], ((0, 0), (0, LANES - dm.H))),
                                jnp.pad(parts["dt"][:, dm.H:], ((0, 0), (0, LANES - dm.H)))], axis=1).astype(BF16)
        proj = _matmul(u, w_main, BF16, min(512, dm.Lp), min(1024, dm.d_lru), "in_proj")
        dtp = _matmul(u, w_dt, F32, min(512, dm.Lp), 2 * LANES, "dt_proj")

        h0_lru = [jnp.concatenate([state_lru[:, l, dd], jnp.zeros((dm.Bp, dm.d_lru), F32)], axis=0)
                  .reshape(dm.nseq, 1, dm.d_lru) for dd in range(2)]
        lru_args = lambda dd: (conv_a_w[l], conv_a_b[l].reshape(1, -1), w_r[l, dd].astype(BF16),
                               b_r[l, dd].reshape(1, -1), w_i[l, dd].astype(BF16), b_i[l, dd].reshape(1, -1),
                               lru_lambda[l, dd].reshape(1, -1), h0_lru[dd])
        hs_f, lru_tf = _lru(dm, False, proj, *lru_args(0))
        ya, lru_tb = _lru(dm, True, proj, *lru_args(1), hs_fwd=hs_f)
        lru_states.append(jnp.stack([lru_tf[dm.Bs:, 0], lru_tb[dm.Bs:, 0]], axis=1))

        h0_ssd = [jnp.concatenate([state_ssd[:, l, dd].reshape(dm.Bs, hp, dm.N),
                                   jnp.zeros((dm.Bp, hp, dm.N), F32)], axis=0) for dd in range(2)]
        ssd_args = lambda dd: (conv_b_w[l], conv_b_b[l].reshape(1, -1), _pad_lanes(dt_bias[l, dd]),
                               _pad_lanes(a_log[l, dd]), h0_ssd[dd])
        dsk = jnp.repeat(d_skip[l].astype(F32), dm.P).reshape(1, dm.d_ssd)
        y_f, ssd_tf = _ssd(dm, False, proj, dtp, *ssd_args(0), dsk)
        yb, ssd_tb = _ssd(dm, True, proj, dtp, *ssd_args(1), ssd_norm_g[l].reshape(1, -1), y_fwd=y_f)
        ssd_states.append(jnp.stack([ssd_tf[dm.Bs:], ssd_tb[dm.Bs:]], axis=1)
                          .reshape(dm.Bp, 2, dm.H, dm.P, dm.N))

        yc = _pool(dm, proj, w_pool[l].astype(BF16), pool_scale[l].reshape(1, -1))

        merged = _merge(dm, ya, yb, yc, proj, w_oa[l].astype(BF16), w_ob[l].astype(BF16),
                        w_oc[l].astype(BF16), tm=min(512, dm.Lp))
        x = _outproj(dm, l, merged, w_out[l].astype(BF16), x, mod4, tm=min(512, dm.Lp))

        u2, route = _norm_router(dm, l, x, norm2_g[l].reshape(1, d), mod4, wr_pad, br_col)
        tok, wgt, tile_expert, n_used, slot1, slot2 = _dispatch_plan(dm, route, moe_tm)
        xs = jnp.take(u2, tok, axis=0)
        ys = _experts(dm, xs, w_gate[l].astype(BF16), w_up[l].astype(BF16), w_down[l].astype(BF16), wgt,
                      tile_expert, n_used, moe_tm)
        y1 = jnp.take(ys, slot1, axis=0)
        y2 = jnp.take(ys, slot2, axis=0)
        if l + 1 < dm.depth:
            x, u = _combine_norm(dm, l, x, y1, y2, norm1_g[l + 1].reshape(1, d), mod4)
        else:
            fg = final_g.reshape(1, d)
            out = (_combine_final(dm, l, x, y1, y2, fg, mod4, 0, dm.Ts),
                   _combine_final(dm, l, x, y1, y2, fg, mod4, dm.Ts, dm.Tp))

    y_sample = out[0].reshape(dm.Bs, dm.Ls, d)
    y_prompt = out[1].reshape(dm.Bp, dm.Lp, d)
    new_state_lru = jnp.stack(lru_states, axis=1).astype(x_prompt.dtype)
    new_state_ssd = jnp.stack(ssd_states, axis=1).astype(x_prompt.dtype)
    return (y_prompt, y_sample, new_state_lru, new_state_ssd)
```

```python
import functools
import math

import numpy as np
import jax
import jax.numpy as jnp
from jax import lax
from jax.experimental import pallas as pl
from jax.experimental.pallas import tpu as pltpu

F32 = jnp.float32
BF16 = jnp.bfloat16

EPS = 1e-6
GRID_W = 64
CONV_W = 4
CONV_LEFT = 1
LRU_C = 8.0
POOL_WINDOWS = (2, 4, 8, 16)
N_EXPERT_GROUPS = 4
TOP_K = 2

LANES = 128
HALO = 16
VMEM_LIMIT = 56 * 1024 * 1024


def _cparams(sem):
    return pltpu.CompilerParams(dimension_semantics=sem, vmem_limit_bytes=VMEM_LIMIT)


def _sigmoid_exp(x):
    return 1.0 / (1.0 + jnp.exp(-x))


def _sigmoid(x):
    return 0.5 * jnp.tanh(0.5 * x) + 0.5


def _silu(x):
    return x * _sigmoid(x)


def _softplus(x):
    return jnp.maximum(x, 0.0) + jnp.log(1.0 + jnp.exp(-jnp.abs(x)))


def _gelu_tanh(x):
    return x * (0.5 * (1.0 + jnp.tanh(math.sqrt(2.0 / math.pi) * (x + 0.044715 * (x * x * x)))))


class Dims:
    def __init__(self, x_prompt, x_sample, state_ssd, w_r, w_pool, w_gate, dt_bias, conv_b_w, w_oa, w_ob, w_oc):
        self.Bp, self.Lp, self.D = x_prompt.shape
        self.Bs, self.Ls, _ = x_sample.shape
        self.depth = w_r.shape[0]
        self.Ts = self.Bs * self.Ls
        self.Tp = self.Bp * self.Lp
        self.T = self.Ts + self.Tp
        self.nseq = self.Bs + self.Bp
        self.d_lru = w_oa.shape[1]
        self.lru_blocks = w_r.shape[2]
        self.lru_bw = w_r.shape[3]
        self.d_ssd = w_ob.shape[1]
        self.H = dt_bias.shape[2]
        self.P = self.d_ssd // self.H
        self.N = state_ssd.shape[-1]
        self.d_xbc = conv_b_w.shape[2]
        self.G = (self.d_xbc - self.d_ssd) // (2 * self.N)
        self.R = self.H // self.G
        self.d_pool = w_oc.shape[1]
        self.pool_group = w_pool.shape[2]
        self.E = w_gate.shape[1]
        self.F = w_gate.shape[3]
        self.d_bc = 2 * self.G * self.N
        self.o_xa = 0
        self.o_ga = self.o_xa + self.d_lru
        self.o_z = self.o_ga + self.d_lru
        self.o_xs = self.o_z + self.d_ssd
        self.o_bc = self.o_xs + self.d_ssd
        self.o_dt = self.o_bc + self.d_bc
        self.o_xc = self.o_bc + self.d_bc
        self.o_gate = self.o_xc + self.d_pool
        self.n_main = self.o_gate + 3 * self.D


def _seq_tile_info(dm, tile, tl):
    row0 = tile * tl
    in_sample = row0 < dm.Ts
    rp = jnp.maximum(row0 - dm.Ts, 0)
    pos = jnp.where(in_sample, row0 % dm.Ls, rp % dm.Lp)
    lseq = jnp.where(in_sample, dm.Ls, dm.Lp)
    seq = jnp.where(in_sample, row0 // dm.Ls, dm.Bs + rp // dm.Lp)
    return pos, lseq, seq


def _cond_idx(dm, i, tm):
    return jnp.minimum((i * tm) // dm.Ls, dm.Bs)


def _mod_kernel(c_ref, w_ref, b_ref, o_ref):
    c = c_ref[...]
    a = _silu(c).astype(BF16)
    o_ref[...] = jnp.dot(a, w_ref[...].astype(BF16), preferred_element_type=F32) + b_ref[...]


def _mod_table(cond, w_mod, b_mod, tn=512):
    depth, d, n6 = w_mod.shape
    nc = cond.shape[0]
    return pl.pallas_call(
        _mod_kernel,
        grid=(depth, n6 // tn),
        in_specs=[
            pl.BlockSpec((nc, d), lambda l, j: (0, 0)),
            pl.BlockSpec((None, d, tn), lambda l, j: (l, 0, j)),
            pl.BlockSpec((None, 1, tn), lambda l, j: (l, 0, j)),
        ],
        out_specs=pl.BlockSpec((None, nc, tn), lambda l, j: (l, 0, j)),
        out_shape=jax.ShapeDtypeStruct((depth, nc, n6), F32),
        compiler_params=_cparams(("arbitrary", "arbitrary")),
        name="mod_table",
    )(cond, w_mod, b_mod.reshape(depth, 1, n6))


def _norm_mod(x, g, sc, sh):
    ms = jnp.mean(x * x, axis=-1, keepdims=True)
    y = x * lax.rsqrt(ms + EPS) * g
    return y * (1.0 + sc) + sh


def _mod_spec(dm, layer, section, tm):
    return pl.BlockSpec((None, None, 1, dm.D), lambda i: (layer, _cond_idx(dm, i, tm), 0, section))


def _embed_norm_kernel(dm, tm, xs_ref, xp_ref, pos_ref, g_ref, sc_ref, sh_ref, x_ref, u_ref):
    i = pl.program_id(0)
    n_s = dm.Ts // tm

    @pl.when(i < n_s)
    def _():
        x_ref[...] = xs_ref[...] + pos_ref[...]

    @pl.when(i >= n_s)
    def _():
        x_ref[...] = xp_ref[...]

    u_ref[...] = _norm_mod(x_ref[...], g_ref[...], sc_ref[...], sh_ref[...]).astype(u_ref.dtype)


def _embed_norm(dm, xs2, xp2, pos, g, mod4, tm=256):
    n_s = dm.Ts // tm
    n_l = dm.Ls // tm
    row = pl.BlockSpec((tm, dm.D), lambda i: (i, 0))
    return pl.pallas_call(
        functools.partial(_embed_norm_kernel, dm, tm),
        grid=(dm.T // tm,),
        in_specs=[
            pl.BlockSpec((tm, dm.D), lambda i: (jnp.minimum(i, n_s - 1), 0)),
            pl.BlockSpec((tm, dm.D), lambda i: (jnp.maximum(i - n_s, 0), 0)),
            pl.BlockSpec((tm, dm.D), lambda i: (i % n_l, 0)),
            pl.BlockSpec((1, dm.D), lambda i: (0, 0)),
            _mod_spec(dm, 0, 1, tm),
            _mod_spec(dm, 0, 0, tm),
        ],
        out_specs=[row, row],
        out_shape=[jax.ShapeDtypeStruct((dm.T, dm.D), F32), jax.ShapeDtypeStruct((dm.T, dm.D), BF16)],
        compiler_params=_cparams(("arbitrary",)),
        name="embed_norm",
    )(xs2, xp2, pos, g, mod4, mod4)


def _route_rows(s, sel):
    e_total = s.shape[0]
    per = e_total // N_EXPERT_GROUPS
    assert per == 4 and TOP_K == 2
    srow = [s[e:e + 1, :] for e in range(e_total)]
    vrow = [sel[e:e + 1, :] for e in range(e_total)]
    scores = []
    for gi in range(N_EXPERT_GROUPS):
        a, b, c, d = vrow[4 * gi:4 * gi + 4]
        hi1, lo1 = jnp.maximum(a, b), jnp.minimum(a, b)
        hi2, lo2 = jnp.maximum(c, d), jnp.minimum(c, d)
        top = jnp.maximum(hi1, hi2)
        second = jnp.maximum(jnp.minimum(hi1, hi2), jnp.maximum(lo1, lo2))
        scores.append(top + second)
    g = jnp.zeros_like(scores[0])
    best = scores[0]
    for gi in range(1, N_EXPERT_GROUPS):
        upd = scores[gi] > best
        g = jnp.where(upd, float(gi), g)
        best = jnp.where(upd, scores[gi], best)

    def pick(rows, j):
        out = rows[j]
        for gi in range(1, N_EXPERT_GROUPS):
            out = jnp.where(g == float(gi), rows[4 * gi + j], out)
        return out

    v = [pick(vrow, j) for j in range(4)]
    sv = [pick(srow, j) for j in range(4)]
    i1 = jnp.zeros_like(g)
    b1 = v[0]
    for j in range(1, 4):
        upd = v[j] > b1
        i1 = jnp.where(upd, float(j), i1)
        b1 = jnp.where(upd, v[j], b1)
    i2 = jnp.zeros_like(g)
    b2 = jnp.full_like(g, -jnp.inf)
    for j in range(4):
        upd = jnp.logical_and(i1 != float(j), v[j] > b2)
        i2 = jnp.where(upd, float(j), i2)
        b2 = jnp.where(upd, v[j], b2)

    def pick_idx(idx):
        out = sv[0]
        for j in range(1, 4):
            out = jnp.where(idx == float(j), sv[j], out)
        return out

    s1, s2 = pick_idx(i1), pick_idx(i2)
    tot = s1 + s2
    return g * 4.0 + i1, g * 4.0 + i2, s1 / tot, s2 / tot


def _norm_router_kernel(x_ref, g_ref, sc_ref, sh_ref, wr_ref, br_ref, u_ref, r_ref):
    u = _norm_mod(x_ref[...], g_ref[...], sc_ref[...], sh_ref[...])
    u_hi = u.astype(BF16)
    u_ref[...] = u_hi
    u_lo = (u - u_hi.astype(F32)).astype(BF16)
    w = wr_ref[...]
    w_hi = w.astype(BF16)
    w_lo = (w - w_hi.astype(F32)).astype(BF16)
    logits = (jnp.dot(u_hi, w_hi, preferred_element_type=F32) + jnp.dot(u_lo, w_hi, preferred_element_type=F32)
              + jnp.dot(u_hi, w_lo, preferred_element_type=F32))
    e_total = br_ref.shape[0]
    lt = logits.T[:e_total, :]
    s = _sigmoid_exp(lt)
    sel = s + br_ref[...]
    e1, e2, w1, w2 = _route_rows(s, sel)
    rows = lax.broadcasted_iota(jnp.int32, r_ref.shape, 0)
    r_ref[...] = jnp.where(rows == 0, e1, jnp.where(rows == 1, e2, jnp.where(rows == 2, w1,
                                                                             jnp.where(rows == 3, w2, 0.0))))


def _norm_router(dm, layer, x, g, mod4, wr_pad, br_col, tm=256):
    row = pl.BlockSpec((tm, dm.D), lambda i: (i, 0))
    return pl.pallas_call(
        _norm_router_kernel,
        grid=(dm.T // tm,),
        in_specs=[
            row,
            pl.BlockSpec((1, dm.D), lambda i: (0, 0)),
            _mod_spec(dm, layer, 4, tm),
            _mod_spec(dm, layer, 3, tm),
            pl.BlockSpec(wr_pad.shape, lambda i: (0, 0)),
            pl.BlockSpec(br_col.shape, lambda i: (0, 0)),
        ],
        out_specs=[row, pl.BlockSpec((8, tm), lambda i: (0, i))],
        out_shape=[jax.ShapeDtypeStruct((dm.T, dm.D), BF16), jax.ShapeDtypeStruct((8, dm.T), F32)],
        compiler_params=_cparams(("arbitrary",)),
        name="norm_router",
    )(x, g, mod4, mod4, wr_pad, br_col)


def _moe_residual(x_ref, y1_ref, y2_ref, w1_ref, w2_ref, g2_ref):
    moe = w1_ref[...] * y1_ref[...].astype(F32) + w2_ref[...] * y2_ref[...].astype(F32)
    return x_ref[...] + g2_ref[...] * moe


def _combine_norm_kernel(x_ref, y1_ref, y2_ref, w1_ref, w2_ref, g2_ref, g_ref, sc_ref, sh_ref, xo_ref, u_ref):
    x = _moe_residual(x_ref, y1_ref, y2_ref, w1_ref, w2_ref, g2_ref)
    xo_ref[...] = x
    u_ref[...] = _norm_mod(x, g_ref[...], sc_ref[...], sh_ref[...]).astype(u_ref.dtype)


def _combine_norm(dm, layer, x, y1, y2, w1, w2, g_next, mod4, tm=256):
    row = pl.BlockSpec((tm, dm.D), lambda i: (i, 0))
    col = pl.BlockSpec((tm, 1), lambda i: (i, 0))
    return pl.pallas_call(
        _combine_norm_kernel,
        grid=(dm.T // tm,),
        in_specs=[row, row, row, col, col, _mod_spec(dm, layer, 5, tm), pl.BlockSpec((1, dm.D), lambda i: (0, 0)),
                  _mod_spec(dm, layer + 1, 1, tm), _mod_spec(dm, layer + 1, 0, tm)],
        out_specs=[row, row],
        out_shape=[jax.ShapeDtypeStruct((dm.T, dm.D), F32), jax.ShapeDtypeStruct((dm.T, dm.D), BF16)],
        compiler_params=_cparams(("arbitrary",)),
        name="combine_norm",
    )(x, y1, y2, w1, w2, mod4, g_next, mod4, mod4)


def _combine_final_kernel(x_ref, y1_ref, y2_ref, w1_ref, w2_ref, g2_ref, g_ref, o_ref):
    x = _moe_residual(x_ref, y1_ref, y2_ref, w1_ref, w2_ref, g2_ref)
    ms = jnp.mean(x * x, axis=-1, keepdims=True)
    o_ref[...] = x * lax.rsqrt(ms + EPS) * g_ref[...]


def _combine_final(dm, layer, x, y1, y2, w1, w2, g_final, mod4, row_start, n_rows, tm=256):
    off = row_start // tm
    row = pl.BlockSpec((tm, dm.D), lambda i: (i + off, 0))
    col = pl.BlockSpec((tm, 1), lambda i: (i + off, 0))
    return pl.pallas_call(
        _combine_final_kernel,
        grid=(n_rows // tm,),
        in_specs=[row, row, row, col, col,
                  pl.BlockSpec((None, None, 1, dm.D), lambda i: (layer, _cond_idx(dm, i + off, tm), 0, 5)),
                  pl.BlockSpec((1, dm.D), lambda i: (0, 0))],
        out_specs=pl.BlockSpec((tm, dm.D), lambda i: (i, 0)),
        out_shape=jax.ShapeDtypeStruct((n_rows, dm.D), F32),
        compiler_params=_cparams(("arbitrary",)),
        name="combine_final",
    )(x, y1, y2, w1, w2, mod4, g_final)


def _mm_kernel(a_ref, w_ref, o_ref):
    o_ref[...] = jnp.dot(a_ref[...], w_ref[...], preferred_element_type=F32).astype(o_ref.dtype)


def _matmul(a, w, out_dtype, tm, tn, name):
    m, k = a.shape
    n = w.shape[1]
    return pl.pallas_call(
        _mm_kernel,
        grid=(n // tn, m // tm),
        in_specs=[pl.BlockSpec((tm, k), lambda j, i: (i, 0)), pl.BlockSpec((k, tn), lambda j, i: (0, j))],
        out_specs=pl.BlockSpec((tm, tn), lambda j, i: (i, j)),
        out_shape=jax.ShapeDtypeStruct((m, n), out_dtype),
        compiler_params=_cparams(("arbitrary", "arbitrary")),
        name=name,
    )(a, w)


def _merge_kernel(ya_ref, yb_ref, yc_ref, g0_ref, g1_ref, g2_ref, wa_ref, wb_ref, wc_ref, o_ref):
    acc = _sigmoid(g0_ref[...].astype(F32)) * jnp.dot(ya_ref[...], wa_ref[...], preferred_element_type=F32)
    acc += _sigmoid(g1_ref[...].astype(F32)) * jnp.dot(yb_ref[...], wb_ref[...], preferred_element_type=F32)
    acc += _sigmoid(g2_ref[...].astype(F32)) * jnp.dot(yc_ref[...], wc_ref[...], preferred_element_type=F32)
    o_ref[...] = acc.astype(o_ref.dtype)


def _merge(dm, ya, yb, yc, proj, wa, wb, wc, tm=512, tn=1024):
    tn = min(tn, dm.D)
    nb = dm.D // tn
    gb = dm.o_gate // tn

    def gate_spec(k):
        return pl.BlockSpec((tm, tn), lambda j, i: (i, gb + k * nb + j))

    def a_spec(kd):
        return pl.BlockSpec((tm, kd), lambda j, i: (i, 0))

    def w_spec(kd):
        return pl.BlockSpec((kd, tn), lambda j, i: (0, j))

    return pl.pallas_call(
        _merge_kernel,
        grid=(nb, dm.T // tm),
        in_specs=[a_spec(dm.d_lru), a_spec(dm.d_ssd), a_spec(dm.d_pool), gate_spec(0), gate_spec(1), gate_spec(2),
                  w_spec(dm.d_lru), w_spec(dm.d_ssd), w_spec(dm.d_pool)],
        out_specs=pl.BlockSpec((tm, tn), lambda j, i: (i, j)),
        out_shape=jax.ShapeDtypeStruct((dm.T, dm.D), BF16),
        compiler_params=_cparams(("arbitrary", "arbitrary")),
        name="merge",
    )(ya, yb, yc, proj, proj, proj, wa, wb, wc)


def _outproj_kernel(a_ref, w_ref, x_ref, g_ref, o_ref):
    o_ref[...] = x_ref[...] + g_ref[...] * jnp.dot(a_ref[...], w_ref[...], preferred_element_type=F32)


def _outproj(dm, layer, merged, w_out, x, mod4, tm=512, tn=1024):
    tn = min(tn, dm.D)
    nb = dm.D // tn
    return pl.pallas_call(
        _outproj_kernel,
        grid=(nb, dm.T // tm),
        in_specs=[
            pl.BlockSpec((tm, dm.D), lambda j, i: (i, 0)),
            pl.BlockSpec((dm.D, tn), lambda j, i: (0, j)),
            pl.BlockSpec((tm, tn), lambda j, i: (i, j)),
            pl.BlockSpec((None, None, 1, tn), lambda j, i: (layer, _cond_idx(dm, i, tm), 0, 2 * nb + j)),
        ],
        out_specs=pl.BlockSpec((tm, tn), lambda j, i: (i, j)),
        out_shape=jax.ShapeDtypeStruct((dm.T, dm.D), F32),
        compiler_params=_cparams(("arbitrary", "arbitrary")),
        name="outproj",
    )(merged, w_out, x, mod4)


def _halo_specs(dm, tl, width, col_block, tile_of):
    per = tl // HALO
    last = dm.T // HALO - 1
    cur = pl.BlockSpec((tl, width), lambda i: (tile_of(i), col_block))
    prev = pl.BlockSpec((HALO, width), lambda i: (jnp.maximum(tile_of(i) * per - 1, 0), col_block))
    nxt = pl.BlockSpec((HALO, width), lambda i: (jnp.minimum((tile_of(i) + 1) * per, last), col_block))
    return [cur, prev, nxt]


def _fill_ext(ext_ref, cur_ref, prev_ref, next_ref, is_first, is_last, tl):
    ext_ref[pl.ds(HALO, tl), :] = cur_ref[...].astype(F32)
    ext_ref[pl.ds(0, HALO), :] = jnp.where(is_first, 0.0, prev_ref[...].astype(F32))
    ext_ref[pl.ds(HALO + tl, HALO), :] = jnp.where(is_last, 0.0, next_ref[...].astype(F32))


def _conv_from_ext(ext_ref, w_ref, b_ref, tl):
    y = b_ref[...] + ext_ref[pl.ds(HALO - CONV_LEFT, tl), :] * w_ref[0:1, :]
    for k in range(1, CONV_W):
        y = y + ext_ref[pl.ds(HALO - CONV_LEFT + k, tl), :] * w_ref[k:k + 1, :]
    return y


def _lru_kernel(dm, tl, reverse, *refs):
    if reverse:
        (xa_ref, xp_ref, xn_ref, cw_ref, cb_ref, wr_ref, br_ref, wi_ref, bi_ref, lam_ref, h0_ref,
         ga_ref, hf_ref, y_ref, hT_ref, ext_ref, a_ref, bx_ref, hs_ref, h_ref) = refs
    else:
        (xa_ref, xp_ref, xn_ref, cw_ref, cb_ref, wr_ref, br_ref, wi_ref, bi_ref, lam_ref, h0_ref,
         y_ref, hT_ref, ext_ref, a_ref, bx_ref, hs_ref, h_ref) = refs
    nt = dm.T // tl
    i = pl.program_id(0)
    tile = nt - 1 - i if reverse else i
    pos, lseq, _ = _seq_tile_info(dm, tile, tl)
    is_first = pos == 0
    is_last = pos + tl == lseq
    _fill_ext(ext_ref, xa_ref, xp_ref, xn_ref, is_first, is_last, tl)
    xc = _conv_from_ext(ext_ref, cw_ref, cb_ref, tl)
    sp = _softplus(-lam_ref[...])
    bw = dm.lru_bw
    for n in range(dm.lru_blocks):
        cs = slice(n * bw, (n + 1) * bw)
        xb = xc[:, cs]
        xbh = xb.astype(BF16)
        r = _sigmoid(jnp.dot(xbh, wr_ref[n], preferred_element_type=F32) + br_ref[:, cs])
        ig = _sigmoid(jnp.dot(xbh, wi_ref[n], preferred_element_type=F32) + bi_ref[:, cs])
        log_a = (-LRU_C) * r * sp[:, cs]
        a = jnp.exp(log_a)
        a_ref[:, cs] = a
        bx_ref[:, cs] = jnp.sqrt(1.0 - a * a) * (ig * xb)

    start_of_scan = is_last if reverse else is_first

    @pl.when(start_of_scan)
    def _():
        h_ref[...] = h0_ref[...]

    def body(t, h):
        tt = tl - 1 - t if reverse else t
        h = a_ref[pl.ds(tt, 1), :] * h + bx_ref[pl.ds(tt, 1), :]
        hs_ref[pl.ds(tt, 1), :] = h
        return h

    h = lax.fori_loop(0, tl, body, h_ref[...], unroll=8)
    h_ref[...] = h
    hT_ref[...] = h
    if reverse:
        y = (hf_ref[...].astype(F32) + hs_ref[...]) * _gelu_tanh(ga_ref[...].astype(F32))
        y_ref[...] = y.astype(y_ref.dtype)
    else:
        y_ref[...] = hs_ref[...].astype(y_ref.dtype)


def _lru(dm, reverse, proj, cw, cb, wr, br, wi, bi, lam, h0, hs_fwd=None, tl=256):
    nt = dm.T // tl
    c = dm.d_lru
    tile_of = (lambda i: nt - 1 - i) if reverse else (lambda i: i)
    seq_of = lambda i: _seq_tile_info(dm, tile_of(i), tl)[2]
    full = lambda shape: pl.BlockSpec(shape, lambda i: (0,) * len(shape))
    row = pl.BlockSpec((tl, c), lambda i: (tile_of(i), 0))
    in_specs = _halo_specs(dm, tl, c, dm.o_xa // c, tile_of) + [
        full((CONV_W, c)), full((1, c)),
        full(wr.shape), full((1, c)), full(wi.shape), full((1, c)), full((1, c)),
        pl.BlockSpec((None, 1, c), lambda i: (seq_of(i), 0, 0)),
    ]
    args = [proj, proj, proj, cw, cb, wr, br, wi, bi, lam, h0]
    if reverse:
        in_specs += [pl.BlockSpec((tl, c), lambda i: (tile_of(i), dm.o_ga // c)), row]
        args += [proj, hs_fwd]
    return pl.pallas_call(
        functools.partial(_lru_kernel, dm, tl, reverse),
        grid=(nt,),
        in_specs=in_specs,
        out_specs=[row, pl.BlockSpec((None, 1, c), lambda i: (seq_of(i), 0, 0))],
        out_shape=[jax.ShapeDtypeStruct((dm.T, c), BF16), jax.ShapeDtypeStruct((dm.nseq, 1, c), F32)],
        scratch_shapes=[pltpu.VMEM((tl + 2 * HALO, c), F32), pltpu.VMEM((tl, c), F32), pltpu.VMEM((tl, c), F32),
                        pltpu.VMEM((tl, c), F32), pltpu.VMEM((1, c), F32)],
        compiler_params=_cparams(("arbitrary",)),
        name="lru_bwd" if reverse else "lru_fwd",
    )(*args)


def _pool_kernel(dm, tl, xc_ref, xp_ref, xn_ref, w_ref, s_ref, o_ref, ext_ref):
    tile = pl.program_id(0)
    pos, lseq, _ = _seq_tile_info(dm, tile, tl)
    _fill_ext(ext_ref, xc_ref, xp_ref, xn_ref, pos == 0, pos + tl == lseq, tl)
    t = pos + lax.broadcasted_iota(jnp.int32, (tl, 1), 0)
    pg = dm.pool_group
    for k, w in enumerate(POOL_WINDOWS):
        cs = slice(k * pg, (k + 1) * pg)
        acc = ext_ref[pl.ds(HALO - w // 2, tl), cs]
        for o in range(-w // 2 + 1, w // 2):
            acc = acc + ext_ref[pl.ds(HALO + o, tl), cs]
        cnt = (jnp.minimum(t + w // 2, lseq) - jnp.maximum(t - w // 2, 0)).astype(F32)
        dev = acc / cnt - ext_ref[pl.ds(HALO, tl), cs]
        y = jnp.dot(dev.astype(BF16), w_ref[k], preferred_element_type=F32)
        o_ref[:, cs] = (y * s_ref[:, cs]).astype(o_ref.dtype)


def _pool(dm, proj, w_pool, scale, tl=256):
    c = dm.d_pool
    return pl.pallas_call(
        functools.partial(_pool_kernel, dm, tl),
        grid=(dm.T // tl,),
        in_specs=_halo_specs(dm, tl, c, dm.o_xc // c, lambda i: i) + [
            pl.BlockSpec(w_pool.shape, lambda i: (0, 0, 0)), pl.BlockSpec((1, c), lambda i: (0, 0))],
        out_specs=pl.BlockSpec((tl, c), lambda i: (i, 0)),
        out_shape=jax.ShapeDtypeStruct((dm.T, c), BF16),
        scratch_shapes=[pltpu.VMEM((tl + 2 * HALO, c), F32)],
        compiler_params=_cparams(("arbitrary",)),
        name="pool",
    )(proj, proj, proj, w_pool, scale)


def _dot_exact(a, b):
    return jnp.dot(a, b, preferred_element_type=F32, precision=lax.Precision.HIGHEST)


def _ssd_kernel(dm, q, reverse, *refs):
    if reverse:
        (xact_ref, dt_ref, dtb_ref, alog_ref, h0_ref, z_ref, yf_ref, ng_ref,
         y_ref, hT_ref, h_ref, yacc_ref) = refs
    else:
        (xs_ref, xsp_ref, xsn_ref, bc_ref, bcp_ref, bcn_ref, dt_ref, cwx_ref, cbx_ref, cwb_ref, cbb_ref,
         dtb_ref, alog_ref, h0_ref, dsk_ref, y_ref, hT_ref, xact_ref, extx_ref, extb_ref, h_ref) = refs
    nt = dm.T // q
    i = pl.program_id(0)
    tile = nt - 1 - i if reverse else i
    pos, lseq, _ = _seq_tile_info(dm, tile, q)
    is_first = pos == 0
    is_last = pos + q == lseq
    d_ssd, n, g_cnt, r_cnt, p = dm.d_ssd, dm.N, dm.G, dm.R, dm.P
    gn = g_cnt * n
    pair = 2 * p
    assert pair == LANES and n == LANES and q == LANES
    if reverse:
        xs = xact_ref[:, :d_ssd].astype(F32)
        bc = xact_ref[:, d_ssd:].astype(F32)
    else:
        _fill_ext(extx_ref, xs_ref, xsp_ref, xsn_ref, is_first, is_last, q)
        _fill_ext(extb_ref, bc_ref, bcp_ref, bcn_ref, is_first, is_last, q)
        xs = _silu(_conv_from_ext(extx_ref, cwx_ref, cbx_ref, q))
        bc = _silu(_conv_from_ext(extb_ref, cwb_ref, cbb_ref, q))
        xact_ref[:, :d_ssd] = xs.astype(xact_ref.dtype)
        xact_ref[:, d_ssd:] = bc.astype(xact_ref.dtype)

    @pl.when(is_last if reverse else is_first)
    def _():
        h_ref[...] = h0_ref[...]

    dt = _softplus(dt_ref[...] + dtb_ref[...])
    a_neg = -jnp.exp(alog_ref[...])
    dta = dt * a_neg
    ri = lax.broadcasted_iota(jnp.int32, (q, q), 0)
    ci = lax.broadcasted_iota(jnp.int32, (q, q), 1)
    valid = (ci >= ri) if reverse else (ci <= ri)
    a_cs = _dot_exact(valid.astype(F32), dta)
    a_end = a_cs[0:1, :] if reverse else a_cs[q - 1:q, :]
    w_state = dt * jnp.exp(a_end - a_cs)
    src_t = (a_cs - jnp.log(dt)).T
    cd_t = jnp.broadcast_to(jnp.exp(a_end), (q, LANES)).T
    lane = lax.broadcasted_iota(jnp.int32, (q, LANES), 1)
    low_half = lane < p

    for g in range(g_cnt):
        b_h = bc[:, g * n:(g + 1) * n].astype(BF16)
        c_h = bc[:, gn + g * n:gn + (g + 1) * n].astype(BF16)
        scores = lax.dot_general(c_h, b_h, (((1,), (1,)), ((), ())), preferred_element_type=F32)
        r0 = g * r_cnt * p
        h_grp = h_ref[pl.ds(r0, r_cnt * p), :].astype(BF16)
        y_off = lax.dot_general(c_h, h_grp, (((1,), (1,)), ((), ())), preferred_element_type=F32)
        for k in range(r_cnt // 2):
            c0 = r0 + 2 * k * p
            hd0 = g * r_cnt + 2 * k
            xs_pair = xs[:, c0:c0 + pair]
            xs_h = xs_pair.astype(BF16)
            ys, eas, ws = [], [], []
            for s in range(2):
                hd = hd0 + s
                col = jnp.broadcast_to(a_cs[:, hd:hd + 1], (q, q))
                row = jnp.broadcast_to(src_t[hd:hd + 1, :], (q, q))
                m = (scores * jnp.exp(jnp.where(valid, col - row, -jnp.inf))).astype(BF16)
                ys.append(jnp.dot(m, xs_h, preferred_element_type=F32))
                eas.append(jnp.exp(col))
                ws.append(jnp.broadcast_to(w_state[:, hd:hd + 1], (q, LANES)))
            y_pair = (jnp.where(low_half, ys[0], ys[1])
                      + y_off[:, c0 - r0:c0 - r0 + pair] * jnp.where(low_half, eas[0], eas[1]))
            xsw = (xs_pair * jnp.where(low_half, ws[0], ws[1])).astype(BF16)
            st = lax.dot_general(xsw, b_h, (((0,), (0,)), ((), ())), preferred_element_type=F32)
            dec = jnp.concatenate([jnp.broadcast_to(cd_t[hd0:hd0 + 1, :], (p, n)),
                                   jnp.broadcast_to(cd_t[hd0 + 1:hd0 + 2, :], (p, n))], axis=0)
            h_ref[pl.ds(c0, pair), :] = dec * h_ref[pl.ds(c0, pair), :] + st
            if reverse:
                yacc_ref[:, c0:c0 + pair] = y_pair
            else:
                y_ref[:, c0:c0 + pair] = (y_pair + dsk_ref[:, c0:c0 + pair] * xs_pair).astype(y_ref.dtype)

    hT_ref[...] = h_ref[...]
    if reverse:
        gw = d_ssd // g_cnt
        for g in range(g_cnt):
            cs = slice(g * gw, (g + 1) * gw)
            y = (yacc_ref[:, cs] + yf_ref[:, cs]) * _silu(z_ref[:, cs].astype(F32))
            ms = jnp.mean(y * y, axis=-1, keepdims=True)
            y_ref[:, cs] = (y * lax.rsqrt(ms + EPS) * ng_ref[:, cs]).astype(y_ref.dtype)


def _ssd(dm, reverse, proj, dtp, cw, cb, dtb, alog, h0, extra, fwd=None, q=128):
    nt = dm.T // q
    tile_of = (lambda i: nt - 1 - i) if reverse else (lambda i: i)
    seq_of = lambda i: _seq_tile_info(dm, tile_of(i), q)[2]
    full = lambda shape: pl.BlockSpec(shape, lambda i: (0,) * len(shape))
    hp = dm.H * dm.P
    rowy = pl.BlockSpec((q, dm.d_ssd), lambda i: (tile_of(i), 0))
    rowact = pl.BlockSpec((q, dm.d_xbc), lambda i: (tile_of(i), 0))
    state = pl.BlockSpec((None, hp, dm.N), lambda i: (seq_of(i), 0, 0))
    dt_spec = pl.BlockSpec((q, LANES), lambda i: (tile_of(i), 1 if reverse else 0))
    vec = [full((1, LANES)), full((1, LANES)), state]
    if reverse:
        y_fwd, xact = fwd
        in_specs = [rowact, dt_spec] + vec + [
            pl.BlockSpec((q, dm.d_ssd), lambda i: (tile_of(i), dm.o_z // dm.d_ssd)), rowy, full((1, dm.d_ssd))]
        args = [xact, dtp, dtb, alog, h0, proj, y_fwd, extra]
        out_specs = [rowy, state]
        out_shape = [jax.ShapeDtypeStruct((dm.T, dm.d_ssd), BF16), jax.ShapeDtypeStruct((dm.nseq, hp, dm.N), F32)]
        scratch = [pltpu.VMEM((hp, dm.N), F32), pltpu.VMEM((q, dm.d_ssd), F32)]
    else:
        in_specs = (_halo_specs(dm, q, dm.d_ssd, dm.o_xs // dm.d_ssd, tile_of)
                    + _halo_specs(dm, q, dm.d_bc, dm.o_bc // dm.d_bc, tile_of)
                    + [dt_spec, full((CONV_W, dm.d_ssd)), full((1, dm.d_ssd)), full((CONV_W, dm.d_bc)),
                       full((1, dm.d_bc))] + vec + [full((1, dm.d_ssd))])
        args = [proj] * 6 + [dtp, cw[:, :dm.d_ssd], cb[:, :dm.d_ssd], cw[:, dm.d_ssd:], cb[:, dm.d_ssd:],
                             dtb, alog, h0, extra]
        out_specs = [rowy, state, rowact]
        out_shape = [jax.ShapeDtypeStruct((dm.T, dm.d_ssd), F32), jax.ShapeDtypeStruct((dm.nseq, hp, dm.N), F32),
                     jax.ShapeDtypeStruct((dm.T, dm.d_xbc), BF16)]
        scratch = [pltpu.VMEM((q + 2 * HALO, dm.d_ssd), F32), pltpu.VMEM((q + 2 * HALO, dm.d_bc), F32),
                   pltpu.VMEM((hp, dm.N), F32)]
    return pl.pallas_call(
        functools.partial(_ssd_kernel, dm, q, reverse),
        grid=(nt,),
        in_specs=in_specs,
        out_specs=out_specs,
        out_shape=out_shape,
        scratch_shapes=scratch,
        compiler_params=_cparams(("arbitrary",)),
        name="ssd_bwd" if reverse else "ssd_fwd",
    )(*args)


def _expert_changed(te_ref, i):
    return jnp.logical_or(i == 0, te_ref[i] != te_ref[jnp.maximum(i - 1, 0)])


def _expert_up_kernel(te_ref, nu_ref, x_ref, wg_ref, wu_ref, h_ref, wg_s, wu_s):
    i = pl.program_id(1)

    @pl.when(jnp.logical_and(i < nu_ref[0], _expert_changed(te_ref, i)))
    def _():
        wg_s[...] = wg_ref[...].astype(BF16)
        wu_s[...] = wu_ref[...].astype(BF16)

    @pl.when(i < nu_ref[0])
    def _():
        x = x_ref[...]
        a = jnp.dot(x, wg_s[...], preferred_element_type=F32)
        b = jnp.dot(x, wu_s[...], preferred_element_type=F32)
        h_ref[...] = (_silu(a) * b).astype(h_ref.dtype)


def _expert_down_kernel(te_ref, nu_ref, h_ref, wd_ref, y_ref, wd_s):
    i = pl.program_id(1)

    @pl.when(jnp.logical_and(i < nu_ref[0], _expert_changed(te_ref, i)))
    def _():
        wd_s[...] = wd_ref[...].astype(BF16)

    @pl.when(i < nu_ref[0])
    def _():
        y_ref[...] = jnp.dot(h_ref[...], wd_s[...], preferred_element_type=F32).astype(y_ref.dtype)


def _experts(dm, layer, xs, w_gate, w_up, w_down, tile_expert, n_used, tm, n_split=2):
    p_pad = xs.shape[0]
    n_tiles = p_pad // tm
    d, f = dm.D, dm.F
    fh, dh = f // n_split, d // n_split
    single = pl.Buffered(1)
    h = pl.pallas_call(
        _expert_up_kernel,
        grid_spec=pltpu.PrefetchScalarGridSpec(
            num_scalar_prefetch=2, grid=(n_split, n_tiles),
            in_specs=[pl.BlockSpec((tm, d), lambda j, i, te, nu: (i, 0)),
                      pl.BlockSpec((None, None, d, fh), lambda j, i, te, nu: (layer, te[i], 0, j),
                                   pipeline_mode=single),
                      pl.BlockSpec((None, None, d, fh), lambda j, i, te, nu: (layer, te[i], 0, j),
                                   pipeline_mode=single)],
            out_specs=pl.BlockSpec((tm, fh), lambda j, i, te, nu: (i, j)),
            scratch_shapes=[pltpu.VMEM((d, fh), BF16), pltpu.VMEM((d, fh), BF16)]),
        out_shape=jax.ShapeDtypeStruct((p_pad, f), BF16),
        compiler_params=_cparams(("arbitrary", "arbitrary")),
        name="expert_up",
    )(tile_expert, n_used, xs, w_gate, w_up)
    return pl.pallas_call(
        _expert_down_kernel,
        grid_spec=pltpu.PrefetchScalarGridSpec(
            num_scalar_prefetch=2, grid=(n_split, n_tiles),
            in_specs=[pl.BlockSpec((tm, f), lambda j, i, te, nu: (i, 0)),
                      pl.BlockSpec((None, None, f, dh), lambda j, i, te, nu: (layer, te[i], 0, j))],
            out_specs=pl.BlockSpec((tm, dh), lambda j, i, te, nu: (i, j)),
            scratch_shapes=[pltpu.VMEM((f, dh), BF16)]),
        out_shape=jax.ShapeDtypeStruct((p_pad, d), BF16),
        compiler_params=_cparams(("arbitrary", "arbitrary")),
        name="expert_down",
    )(tile_expert, n_used, h, w_down)


def _dispatch_plan(dm, route, tm):
    t = dm.T
    e_cnt = dm.E
    e_idx = route[0:2].astype(jnp.int32).reshape(-1)
    onehot = (e_idx[:, None] == jnp.arange(e_cnt, dtype=jnp.int32)[None, :]).astype(jnp.int32)
    rank = jnp.sum((jnp.cumsum(onehot, axis=0) - onehot) * onehot, axis=1)
    counts = jnp.sum(onehot, axis=0)
    tiles_per = (counts + tm - 1) // tm
    tile_end = jnp.cumsum(tiles_per)
    start = (tile_end - tiles_per) * tm
    slot = jnp.sum(onehot * start[None, :], axis=1) + rank
    n_tiles = (TOP_K * t) // tm + e_cnt
    p_pad = n_tiles * tm
    tok = jnp.zeros((p_pad,), jnp.int32).at[slot].set(jnp.tile(jnp.arange(t, dtype=jnp.int32), TOP_K),
                                                      unique_indices=True, mode="promise_in_bounds")
    tile_ids = jnp.arange(n_tiles, dtype=jnp.int32)
    tile_expert = jnp.minimum(jnp.sum((tile_end[None, :] <= tile_ids[:, None]).astype(jnp.int32), axis=1),
                              e_cnt - 1)
    n_used = tile_end[-1:].astype(jnp.int32)
    return tok, tile_expert, n_used, slot[:t], slot[t:]


def _take_rows(a, idx):
    return a.at[idx].get(mode="promise_in_bounds")


def _grid_pos_embed(n_tokens, d_model):
    rows = n_tokens // GRID_W
    row = jnp.repeat(jnp.arange(rows), GRID_W).astype(F32)
    col = jnp.tile(jnp.arange(GRID_W), rows).astype(F32)
    quarter = d_model // 4
    omega = 1.0 / (10000.0 ** (jnp.arange(quarter, dtype=F32) / quarter))
    ang_r = row[:, None] * omega[None, :]
    ang_c = col[:, None] * omega[None, :]
    return jnp.concatenate([jnp.sin(ang_r), jnp.cos(ang_r), jnp.sin(ang_c), jnp.cos(ang_c)], axis=-1)


def _pad_lanes(v, fill=0.0):
    return jnp.pad(v.astype(F32), (0, LANES - v.shape[0]), constant_values=fill).reshape(1, LANES)


def kernel(x_prompt, x_sample, state_lru, state_ssd, c, c_ctx, w_mod, b_mod, norm1_g, w_in, conv_a_w, conv_a_b, w_r, b_r, w_i, b_i, lru_lambda, w_oa, conv_b_w, conv_b_b, dt_bias, a_log, d_skip, ssd_norm_g, w_ob, w_pool, pool_scale, w_oc, w_out, norm2_g, w_router, b_router, w_gate, w_up, w_down, final_g):
    dm = Dims(x_prompt, x_sample, state_ssd, w_r, w_pool, w_gate, dt_bias, conv_b_w, w_oa, w_ob, w_oc)
    d = dm.D
    moe_tm = min(512, dm.Lp)

    ncond = 16
    cond = jnp.zeros((ncond, d), F32).at[:dm.Bs].set(c).at[dm.Bs].set(c_ctx)
    mod4 = _mod_table(cond, w_mod, b_mod).reshape(dm.depth, ncond, 1, 6 * d)

    pos = _grid_pos_embed(dm.Ls, d)
    x, u = _embed_norm(dm, x_sample.reshape(dm.Ts, d), x_prompt.reshape(dm.Tp, d), pos,
                       norm1_g[0].reshape(1, d), mod4)

    wr_pad = jnp.pad(w_router, ((0, 0), (0, LANES - dm.E)))
    br_col = b_router.reshape(dm.E, 1).astype(F32)
    hp = dm.H * dm.P
    lru_states, ssd_states = [], []
    out = None
    for l in range(dm.depth):
        wl = w_in[l]
        w_main = jnp.concatenate([wl[:, :dm.o_dt], wl[:, dm.o_dt + 2 * dm.H:]], axis=1).astype(BF16)
        lane_pad = ((0, 0), (0, LANES - dm.H))
        w_dt = jnp.concatenate([jnp.pad(wl[:, dm.o_dt:dm.o_dt + dm.H], lane_pad),
                                jnp.pad(wl[:, dm.o_dt + dm.H:dm.o_dt + 2 * dm.H], lane_pad)], axis=1).astype(BF16)
        proj = _matmul(u, w_main, BF16, min(1024, dm.Lp), min(1024, dm.d_lru), "in_proj")
        dtp = _matmul(u, w_dt, F32, min(512, dm.Lp), 2 * LANES, "dt_proj")

        h0_lru = [jnp.concatenate([state_lru[:, l, dd], jnp.zeros((dm.Bp, dm.d_lru), F32)], axis=0)
                  .reshape(dm.nseq, 1, dm.d_lru) for dd in range(2)]
        lru_args = lambda dd: (conv_a_w[l], conv_a_b[l].reshape(1, -1), w_r[l, dd].astype(BF16),
                               b_r[l, dd].reshape(1, -1), w_i[l, dd].astype(BF16), b_i[l, dd].reshape(1, -1),
                               lru_lambda[l, dd].reshape(1, -1), h0_lru[dd])
        hs_f, lru_tf = _lru(dm, False, proj, *lru_args(0))
        ya, lru_tb = _lru(dm, True, proj, *lru_args(1), hs_fwd=hs_f)
        lru_states.append(jnp.stack([lru_tf[dm.Bs:, 0], lru_tb[dm.Bs:, 0]], axis=1))

        h0_ssd = [jnp.concatenate([state_ssd[:, l, dd].reshape(dm.Bs, hp, dm.N),
                                   jnp.zeros((dm.Bp, hp, dm.N), F32)], axis=0) for dd in range(2)]
        ssd_args = lambda dd: (conv_b_w[l], conv_b_b[l].reshape(1, -1), _pad_lanes(dt_bias[l, dd]),
                               _pad_lanes(a_log[l, dd]), h0_ssd[dd])
        dsk = jnp.repeat(d_skip[l].astype(F32), dm.P).reshape(1, dm.d_ssd)
        y_f, ssd_tf, xact = _ssd(dm, False, proj, dtp, *ssd_args(0), dsk)
        yb, ssd_tb = _ssd(dm, True, proj, dtp, *ssd_args(1), ssd_norm_g[l].reshape(1, -1), fwd=(y_f, xact))
        ssd_states.append(jnp.stack([ssd_tf[dm.Bs:], ssd_tb[dm.Bs:]], axis=1)
                          .reshape(dm.Bp, 2, dm.H, dm.P, dm.N))

        yc = _pool(dm, proj, w_pool[l].astype(BF16), pool_scale[l].reshape(1, -1))

        merged = _merge(dm, ya, yb, yc, proj, w_oa[l].astype(BF16), w_ob[l].astype(BF16),
                        w_oc[l].astype(BF16), tm=min(512, dm.Lp))
        x = _outproj(dm, l, merged, w_out[l].astype(BF16), x, mod4, tm=min(512, dm.Lp))

        u2, route = _norm_router(dm, l, x, norm2_g[l].reshape(1, d), mod4, wr_pad, br_col)
        tok, tile_expert, n_used, slot1, slot2 = _dispatch_plan(dm, route, moe_tm)
        ys = _experts(dm, l, _take_rows(u2, tok), w_gate, w_up, w_down, tile_expert, n_used, moe_tm)
        y1 = _take_rows(ys, slot1)
        y2 = _take_rows(ys, slot2)
        w1 = route[2].reshape(dm.T, 1)
        w2 = route[3].reshape(dm.T, 1)
        if l + 1 < dm.depth:
            x, u = _combine_norm(dm, l, x, y1, y2, w1, w2, norm1_g[l + 1].reshape(1, d), mod4)
        else:
            fg = final_g.reshape(1, d)
            out = (_combine_final(dm, l, x, y1, y2, w1, w2, fg, mod4, 0, dm.Ts),
                   _combine_final(dm, l, x, y1, y2, w1, w2, fg, mod4, dm.Ts, dm.Tp))

    y_sample = out[0].reshape(dm.Bs, dm.Ls, d)
    y_prompt = out[1].reshape(dm.Bp, dm.Lp, d)
    new_state_lru = jnp.stack(lru_states, axis=1).astype(x_prompt.dtype)
    new_state_ssd = jnp.stack(ssd_states, axis=1).astype(x_prompt.dtype)
    return (y_prompt, y_sample, new_state_lru, new_state_ssd)
```

```python
import functools
import math

import numpy as np
import jax
import jax.numpy as jnp
from jax import lax
from jax.experimental import pallas as pl
from jax.experimental.pallas import tpu as pltpu

F32 = jnp.float32
BF16 = jnp.bfloat16

EPS = 1e-6
GRID_W = 64
CONV_W = 4
CONV_LEFT = 1
LRU_C = 8.0
POOL_WINDOWS = (2, 4, 8, 16)
N_EXPERT_GROUPS = 4
TOP_K = 2

LANES = 128
HALO = 16
VMEM_LIMIT = 56 * 1024 * 1024


def _cparams(sem):
    return pltpu.CompilerParams(dimension_semantics=sem, vmem_limit_bytes=VMEM_LIMIT)


def _sigmoid_exp(x):
    return 1.0 / (1.0 + jnp.exp(-x))


def _sigmoid(x):
    return 0.5 * jnp.tanh(0.5 * x) + 0.5


def _silu(x):
    return x * _sigmoid(x)


def _softplus(x):
    return jnp.maximum(x, 0.0) + jnp.log(1.0 + jnp.exp(-jnp.abs(x)))


def _gelu_tanh(x):
    return x * (0.5 * (1.0 + jnp.tanh(math.sqrt(2.0 / math.pi) * (x + 0.044715 * (x * x * x)))))


def _pack_halves(x):
    c = x.shape[1] // 2
    lo = lax.bitcast_convert_type(x[:, :c].astype(BF16).astype(F32), jnp.uint32)
    hi = lax.bitcast_convert_type(x[:, c:].astype(BF16).astype(F32), jnp.uint32)
    return (hi & jnp.uint32(0xFFFF0000)) | (lo >> 16)


def _unpack_halves(w):
    lo = lax.bitcast_convert_type(w << 16, F32)
    hi = lax.bitcast_convert_type(w & jnp.uint32(0xFFFF0000), F32)
    return lo, hi


class Dims:
    def __init__(self, x_prompt, x_sample, state_ssd, w_r, w_pool, w_gate, dt_bias, conv_b_w, w_oa, w_ob, w_oc):
        self.Bp, self.Lp, self.D = x_prompt.shape
        self.Bs, self.Ls, _ = x_sample.shape
        self.depth = w_r.shape[0]
        self.Ts = self.Bs * self.Ls
        self.Tp = self.Bp * self.Lp
        self.T = self.Ts + self.Tp
        self.nseq = self.Bs + self.Bp
        self.d_lru = w_oa.shape[1]
        self.lru_blocks = w_r.shape[2]
        self.lru_bw = w_r.shape[3]
        self.d_ssd = w_ob.shape[1]
        self.H = dt_bias.shape[2]
        self.P = self.d_ssd // self.H
        self.N = state_ssd.shape[-1]
        self.d_xbc = conv_b_w.shape[2]
        self.G = (self.d_xbc - self.d_ssd) // (2 * self.N)
        self.R = self.H // self.G
        self.d_pool = w_oc.shape[1]
        self.pool_group = w_pool.shape[2]
        self.E = w_gate.shape[1]
        self.F = w_gate.shape[3]
        self.d_bc = 2 * self.G * self.N
        self.o_xa = 0
        self.o_ga = self.o_xa + self.d_lru
        self.o_z = self.o_ga + self.d_lru
        self.o_xs = self.o_z + self.d_ssd
        self.o_bc = self.o_xs + self.d_ssd
        self.o_dt = self.o_bc + self.d_bc
        self.o_xc = self.o_bc + self.d_bc
        self.o_gate = self.o_xc + self.d_pool
        self.n_main = self.o_gate + 3 * self.D


def _seq_tile_info(dm, tile, tl):
    row0 = tile * tl
    in_sample = row0 < dm.Ts
    rp = jnp.maximum(row0 - dm.Ts, 0)
    pos = jnp.where(in_sample, row0 % dm.Ls, rp % dm.Lp)
    lseq = jnp.where(in_sample, dm.Ls, dm.Lp)
    seq = jnp.where(in_sample, row0 // dm.Ls, dm.Bs + rp // dm.Lp)
    return pos, lseq, seq


def _cond_idx(dm, i, tm):
    return jnp.minimum((i * tm) // dm.Ls, dm.Bs)


def _mod_kernel(c_ref, w_ref, b_ref, o_ref):
    c = c_ref[...]
    a = _silu(c).astype(BF16)
    o_ref[...] = jnp.dot(a, w_ref[...].astype(BF16), preferred_element_type=F32) + b_ref[...]


def _mod_table(cond, w_mod, b_mod, tn=512):
    depth, d, n6 = w_mod.shape
    nc = cond.shape[0]
    return pl.pallas_call(
        _mod_kernel,
        grid=(depth, n6 // tn),
        in_specs=[
            pl.BlockSpec((nc, d), lambda l, j: (0, 0)),
            pl.BlockSpec((None, d, tn), lambda l, j: (l, 0, j)),
            pl.BlockSpec((None, 1, tn), lambda l, j: (l, 0, j)),
        ],
        out_specs=pl.BlockSpec((None, nc, tn), lambda l, j: (l, 0, j)),
        out_shape=jax.ShapeDtypeStruct((depth, nc, n6), F32),
        compiler_params=_cparams(("arbitrary", "arbitrary")),
        name="mod_table",
    )(cond, w_mod, b_mod.reshape(depth, 1, n6))


def _norm_mod(x, g, sc, sh):
    ms = jnp.mean(x * x, axis=-1, keepdims=True)
    y = x * lax.rsqrt(ms + EPS) * g
    return y * (1.0 + sc) + sh


def _mod_spec(dm, layer, section, tm):
    return pl.BlockSpec((None, None, 1, dm.D), lambda i: (layer, _cond_idx(dm, i, tm), 0, section))


def _embed_norm_kernel(dm, tm, xs_ref, xp_ref, pos_ref, g_ref, sc_ref, sh_ref, x_ref, u_ref):
    i = pl.program_id(0)
    n_s = dm.Ts // tm

    @pl.when(i < n_s)
    def _():
        x_ref[...] = xs_ref[...] + pos_ref[...]

    @pl.when(i >= n_s)
    def _():
        x_ref[...] = xp_ref[...]

    u_ref[...] = _norm_mod(x_ref[...], g_ref[...], sc_ref[...], sh_ref[...]).astype(u_ref.dtype)


def _embed_norm(dm, xs2, xp2, pos, g, mod4, tm=256):
    n_s = dm.Ts // tm
    n_l = dm.Ls // tm
    row = pl.BlockSpec((tm, dm.D), lambda i: (i, 0))
    return pl.pallas_call(
        functools.partial(_embed_norm_kernel, dm, tm),
        grid=(dm.T // tm,),
        in_specs=[
            pl.BlockSpec((tm, dm.D), lambda i: (jnp.minimum(i, n_s - 1), 0)),
            pl.BlockSpec((tm, dm.D), lambda i: (jnp.maximum(i - n_s, 0), 0)),
            pl.BlockSpec((tm, dm.D), lambda i: (i % n_l, 0)),
            pl.BlockSpec((1, dm.D), lambda i: (0, 0)),
            _mod_spec(dm, 0, 1, tm),
            _mod_spec(dm, 0, 0, tm),
        ],
        out_specs=[row, row],
        out_shape=[jax.ShapeDtypeStruct((dm.T, dm.D), F32), jax.ShapeDtypeStruct((dm.T, dm.D), BF16)],
        compiler_params=_cparams(("arbitrary",)),
        name="embed_norm",
    )(xs2, xp2, pos, g, mod4, mod4)


def _route_rows(s, sel):
    e_total = s.shape[0]
    per = e_total // N_EXPERT_GROUPS
    assert per == 4 and TOP_K == 2
    srow = [s[e:e + 1, :] for e in range(e_total)]
    vrow = [sel[e:e + 1, :] for e in range(e_total)]
    scores = []
    for gi in range(N_EXPERT_GROUPS):
        a, b, c, d = vrow[4 * gi:4 * gi + 4]
        hi1, lo1 = jnp.maximum(a, b), jnp.minimum(a, b)
        hi2, lo2 = jnp.maximum(c, d), jnp.minimum(c, d)
        top = jnp.maximum(hi1, hi2)
        second = jnp.maximum(jnp.minimum(hi1, hi2), jnp.maximum(lo1, lo2))
        scores.append(top + second)
    g = jnp.zeros_like(scores[0])
    best = scores[0]
    for gi in range(1, N_EXPERT_GROUPS):
        upd = scores[gi] > best
        g = jnp.where(upd, float(gi), g)
        best = jnp.where(upd, scores[gi], best)

    def pick(rows, j):
        out = rows[j]
        for gi in range(1, N_EXPERT_GROUPS):
            out = jnp.where(g == float(gi), rows[4 * gi + j], out)
        return out

    v = [pick(vrow, j) for j in range(4)]
    sv = [pick(srow, j) for j in range(4)]
    i1 = jnp.zeros_like(g)
    b1 = v[0]
    for j in range(1, 4):
        upd = v[j] > b1
        i1 = jnp.where(upd, float(j), i1)
        b1 = jnp.where(upd, v[j], b1)
    i2 = jnp.zeros_like(g)
    b2 = jnp.full_like(g, -jnp.inf)
    for j in range(4):
        upd = jnp.logical_and(i1 != float(j), v[j] > b2)
        i2 = jnp.where(upd, float(j), i2)
        b2 = jnp.where(upd, v[j], b2)

    def pick_idx(idx):
        out = sv[0]
        for j in range(1, 4):
            out = jnp.where(idx == float(j), sv[j], out)
        return out

    s1, s2 = pick_idx(i1), pick_idx(i2)
    tot = s1 + s2
    return g * 4.0 + i1, g * 4.0 + i2, s1 / tot, s2 / tot


def _norm_router_kernel(x_ref, g_ref, sc_ref, sh_ref, wr_ref, br_ref, u_ref, r_ref):
    u = _norm_mod(x_ref[...], g_ref[...], sc_ref[...], sh_ref[...])
    u_hi = u.astype(BF16)
    u_ref[...] = _pack_halves(u)
    u_lo = (u - u_hi.astype(F32)).astype(BF16)
    w = wr_ref[...]
    w_hi = w.astype(BF16)
    w_lo = (w - w_hi.astype(F32)).astype(BF16)
    logits = (jnp.dot(u_hi, w_hi, preferred_element_type=F32) + jnp.dot(u_lo, w_hi, preferred_element_type=F32)
              + jnp.dot(u_hi, w_lo, preferred_element_type=F32))
    e_total = br_ref.shape[0]
    lt = logits.T[:e_total, :]
    s = _sigmoid_exp(lt)
    sel = s + br_ref[...]
    e1, e2, w1, w2 = _route_rows(s, sel)
    rows = lax.broadcasted_iota(jnp.int32, r_ref.shape, 0)
    r_ref[...] = jnp.where(rows == 0, e1, jnp.where(rows == 1, e2, jnp.where(rows == 2, w1,
                                                                             jnp.where(rows == 3, w2, 0.0))))


def _norm_router(dm, layer, x, g, mod4, wr_pad, br_col, tm=256):
    row = pl.BlockSpec((tm, dm.D), lambda i: (i, 0))
    return pl.pallas_call(
        _norm_router_kernel,
        grid=(dm.T // tm,),
        in_specs=[
            row,
            pl.BlockSpec((1, dm.D), lambda i: (0, 0)),
            _mod_spec(dm, layer, 4, tm),
            _mod_spec(dm, layer, 3, tm),
            pl.BlockSpec(wr_pad.shape, lambda i: (0, 0)),
            pl.BlockSpec(br_col.shape, lambda i: (0, 0)),
        ],
        out_specs=[pl.BlockSpec((tm, dm.D // 2), lambda i: (i, 0)), pl.BlockSpec((8, tm), lambda i: (0, i))],
        out_shape=[jax.ShapeDtypeStruct((dm.T, dm.D // 2), jnp.uint32), jax.ShapeDtypeStruct((8, dm.T), F32)],
        compiler_params=_cparams(("arbitrary",)),
        name="norm_router",
    )(x, g, mod4, mod4, wr_pad, br_col)


GATHER_UNROLL = 8


def _row_gather_start(src_hbm, idx_ref, idx_base, dst_ref, sem, n_rows, priority):
    def group(g, carry):
        for k in range(GATHER_UNROLL):
            r = g * GATHER_UNROLL + k
            row = idx_ref[idx_base + r]
            pltpu.make_async_copy(src_hbm.at[pl.ds(row, 1), :], dst_ref.at[pl.ds(r, 1), :], sem).start(
                priority=priority)
        return carry

    lax.fori_loop(0, n_rows // GATHER_UNROLL, group, 0)


def _row_gather_wait(src_hbm, dst_ref, sem, n_rows):
    def group(g, carry):
        for k in range(GATHER_UNROLL):
            r = g * GATHER_UNROLL + k
            pltpu.make_async_copy(src_hbm.at[pl.ds(0, 1), :], dst_ref.at[pl.ds(r, 1), :], sem).wait()
        return carry

    lax.fori_loop(0, n_rows // GATHER_UNROLL, group, 0)


def _gathered_moe_rows(s1_ref, s2_ref, ys_hbm, buf, sem, tm, off):
    i = pl.program_id(0)
    n = pl.num_programs(0)
    slot = lax.rem(i, 2)

    def start(step, into):
        base = (step + off) * tm
        _row_gather_start(ys_hbm, s1_ref, base, buf.at[into, 0], sem.at[into], tm, 0)
        _row_gather_start(ys_hbm, s2_ref, base, buf.at[into, 1], sem.at[into], tm, 1)

    @pl.when(i == 0)
    def _():
        start(0, 0)

    @pl.when(i + 1 < n)
    def _():
        start(i + 1, 1 - slot)

    _row_gather_wait(ys_hbm, buf.at[slot, 0], sem.at[slot], tm)
    _row_gather_wait(ys_hbm, buf.at[slot, 1], sem.at[slot], tm)
    return buf[slot, 0], buf[slot, 1]


def _unpack_expert_rows(w, n_split):
    wb = w.shape[1] // n_split
    parts = []
    for j in range(n_split):
        lo, hi = _unpack_halves(w[:, j * wb:(j + 1) * wb])
        parts += [lo, hi]
    return jnp.concatenate(parts, axis=1)


def _moe_residual(s1_ref, s2_ref, ys_hbm, buf, sem, tm, off, n_split, x_ref, w1_ref, w2_ref, g2_ref):
    p1, p2 = _gathered_moe_rows(s1_ref, s2_ref, ys_hbm, buf, sem, tm, off)
    moe = w1_ref[...] * _unpack_expert_rows(p1, n_split) + w2_ref[...] * _unpack_expert_rows(p2, n_split)
    return x_ref[...] + g2_ref[...] * moe


def _combine_norm_kernel(tm, n_split, s1_ref, s2_ref, x_ref, ys_hbm, w1_ref, w2_ref, g2_ref, g_ref, sc_ref, sh_ref,
                         xo_ref, u_ref, buf, sem):
    x = _moe_residual(s1_ref, s2_ref, ys_hbm, buf, sem, tm, 0, n_split, x_ref, w1_ref, w2_ref, g2_ref)
    xo_ref[...] = x
    u_ref[...] = _norm_mod(x, g_ref[...], sc_ref[...], sh_ref[...]).astype(u_ref.dtype)


def _gather_cparams():
    return pltpu.CompilerParams(dimension_semantics=("arbitrary",), vmem_limit_bytes=VMEM_LIMIT,
                                disable_bounds_checks=True)


def _combine_scratch(dm, tm):
    return [pltpu.VMEM((2, TOP_K, tm, dm.D // 2), jnp.uint32), pltpu.SemaphoreType.DMA((2,))]


def _combine_norm(dm, layer, x, ysp, slot1, slot2, w1, w2, g_next, mod4, n_split, tm=256):
    row = pl.BlockSpec((tm, dm.D), lambda i, s1, s2: (i, 0))
    col = pl.BlockSpec((tm, 1), lambda i, s1, s2: (i, 0))
    vec = pl.BlockSpec((1, dm.D), lambda i, s1, s2: (0, 0))

    def mod(lyr, section):
        return pl.BlockSpec((None, None, 1, dm.D), lambda i, s1, s2: (lyr, _cond_idx(dm, i, tm), 0, section))

    return pl.pallas_call(
        functools.partial(_combine_norm_kernel, tm, n_split),
        grid_spec=pltpu.PrefetchScalarGridSpec(
            num_scalar_prefetch=2, grid=(dm.T // tm,),
            in_specs=[row, pl.BlockSpec(memory_space=pl.ANY), col, col, mod(layer, 5), vec,
                      mod(layer + 1, 1), mod(layer + 1, 0)],
            out_specs=[row, row],
            scratch_shapes=_combine_scratch(dm, tm)),
        out_shape=[jax.ShapeDtypeStruct((dm.T, dm.D), F32), jax.ShapeDtypeStruct((dm.T, dm.D), BF16)],
        compiler_params=_gather_cparams(),
        name="combine_norm",
    )(slot1, slot2, x, ysp, w1, w2, mod4, g_next, mod4, mod4)


def _combine_final_kernel(tm, off, n_split, s1_ref, s2_ref, x_ref, ys_hbm, w1_ref, w2_ref, g2_ref, g_ref, o_ref,
                          buf, sem):
    x = _moe_residual(s1_ref, s2_ref, ys_hbm, buf, sem, tm, off, n_split, x_ref, w1_ref, w2_ref, g2_ref)
    ms = jnp.mean(x * x, axis=-1, keepdims=True)
    o_ref[...] = x * lax.rsqrt(ms + EPS) * g_ref[...]


def _combine_final(dm, layer, x, ysp, slot1, slot2, w1, w2, g_final, mod4, n_split, row_start, n_rows, tm=256):
    off = row_start // tm
    row = pl.BlockSpec((tm, dm.D), lambda i, s1, s2: (i + off, 0))
    col = pl.BlockSpec((tm, 1), lambda i, s1, s2: (i + off, 0))
    return pl.pallas_call(
        functools.partial(_combine_final_kernel, tm, off, n_split),
        grid_spec=pltpu.PrefetchScalarGridSpec(
            num_scalar_prefetch=2, grid=(n_rows // tm,),
            in_specs=[row, pl.BlockSpec(memory_space=pl.ANY), col, col,
                      pl.BlockSpec((None, None, 1, dm.D),
                                   lambda i, s1, s2: (layer, _cond_idx(dm, i + off, tm), 0, 5)),
                      pl.BlockSpec((1, dm.D), lambda i, s1, s2: (0, 0))],
            out_specs=pl.BlockSpec((tm, dm.D), lambda i, s1, s2: (i, 0)),
            scratch_shapes=_combine_scratch(dm, tm)),
        out_shape=jax.ShapeDtypeStruct((n_rows, dm.D), F32),
        compiler_params=_gather_cparams(),
        name="combine_final",
    )(slot1, slot2, x, ysp, w1, w2, mod4, g_final)


def _mm_kernel(a_ref, w_ref, o_ref):
    o_ref[...] = jnp.dot(a_ref[...], w_ref[...], preferred_element_type=F32).astype(o_ref.dtype)


def _matmul(a, w, out_dtype, tm, tn, name):
    m, k = a.shape
    n = w.shape[1]
    return pl.pallas_call(
        _mm_kernel,
        grid=(n // tn, m // tm),
        in_specs=[pl.BlockSpec((tm, k), lambda j, i: (i, 0)), pl.BlockSpec((k, tn), lambda j, i: (0, j))],
        out_specs=pl.BlockSpec((tm, tn), lambda j, i: (i, j)),
        out_shape=jax.ShapeDtypeStruct((m, n), out_dtype),
        compiler_params=_cparams(("arbitrary", "arbitrary")),
        name=name,
    )(a, w)


def _merge_kernel(ya_ref, yb_ref, yc_ref, g0_ref, g1_ref, g2_ref, wa_ref, wb_ref, wc_ref, o_ref):
    acc = _sigmoid(g0_ref[...].astype(F32)) * jnp.dot(ya_ref[...], wa_ref[...], preferred_element_type=F32)
    acc += _sigmoid(g1_ref[...].astype(F32)) * jnp.dot(yb_ref[...], wb_ref[...], preferred_element_type=F32)
    acc += _sigmoid(g2_ref[...].astype(F32)) * jnp.dot(yc_ref[...], wc_ref[...], preferred_element_type=F32)
    o_ref[...] = acc.astype(o_ref.dtype)


def _merge(dm, ya, yb, yc, proj, wa, wb, wc, tm=512, tn=1024):
    tn = min(tn, dm.D)
    nb = dm.D // tn
    gb = dm.o_gate // tn

    def gate_spec(k):
        return pl.BlockSpec((tm, tn), lambda j, i: (i, gb + k * nb + j))

    def a_spec(kd):
        return pl.BlockSpec((tm, kd), lambda j, i: (i, 0))

    def w_spec(kd):
        return pl.BlockSpec((kd, tn), lambda j, i: (0, j))

    return pl.pallas_call(
        _merge_kernel,
        grid=(nb, dm.T // tm),
        in_specs=[a_spec(dm.d_lru), a_spec(dm.d_ssd), a_spec(dm.d_pool), gate_spec(0), gate_spec(1), gate_spec(2),
                  w_spec(dm.d_lru), w_spec(dm.d_ssd), w_spec(dm.d_pool)],
        out_specs=pl.BlockSpec((tm, tn), lambda j, i: (i, j)),
        out_shape=jax.ShapeDtypeStruct((dm.T, dm.D), BF16),
        compiler_params=_cparams(("arbitrary", "arbitrary")),
        name="merge",
    )(ya, yb, yc, proj, proj, proj, wa, wb, wc)


def _outproj_kernel(a_ref, w_ref, x_ref, g_ref, o_ref):
    o_ref[...] = x_ref[...] + g_ref[...] * jnp.dot(a_ref[...], w_ref[...], preferred_element_type=F32)


def _outproj(dm, layer, merged, w_out, x, mod4, tm=512, tn=1024):
    tn = min(tn, dm.D)
    nb = dm.D // tn
    return pl.pallas_call(
        _outproj_kernel,
        grid=(nb, dm.T // tm),
        in_specs=[
            pl.BlockSpec((tm, dm.D), lambda j, i: (i, 0)),
            pl.BlockSpec((dm.D, tn), lambda j, i: (0, j)),
            pl.BlockSpec((tm, tn), lambda j, i: (i, j)),
            pl.BlockSpec((None, None, 1, tn), lambda j, i: (layer, _cond_idx(dm, i, tm), 0, 2 * nb + j)),
        ],
        out_specs=pl.BlockSpec((tm, tn), lambda j, i: (i, j)),
        out_shape=jax.ShapeDtypeStruct((dm.T, dm.D), F32),
        compiler_params=_cparams(("arbitrary", "arbitrary")),
        name="outproj",
    )(merged, w_out, x, mod4)


def _halo_specs(dm, tl, width, col_block, tile_of):
    per = tl // HALO
    last = dm.T // HALO - 1
    cur = pl.BlockSpec((tl, width), lambda i: (tile_of(i), col_block))
    prev = pl.BlockSpec((HALO, width), lambda i: (jnp.maximum(tile_of(i) * per - 1, 0), col_block))
    nxt = pl.BlockSpec((HALO, width), lambda i: (jnp.minimum((tile_of(i) + 1) * per, last), col_block))
    return [cur, prev, nxt]


def _fill_ext(ext_ref, cur_ref, prev_ref, next_ref, is_first, is_last, tl):
    ext_ref[pl.ds(HALO, tl), :] = cur_ref[...].astype(F32)
    ext_ref[pl.ds(0, HALO), :] = jnp.where(is_first, 0.0, prev_ref[...].astype(F32))
    ext_ref[pl.ds(HALO + tl, HALO), :] = jnp.where(is_last, 0.0, next_ref[...].astype(F32))


def _conv_from_ext(ext_ref, w_ref, b_ref, tl):
    y = b_ref[...] + ext_ref[pl.ds(HALO, tl), :] * w_ref[CONV_LEFT:CONV_LEFT + 1, :]
    for k in range(CONV_W):
        if k != CONV_LEFT:
            y = y + ext_ref[pl.ds(HALO - CONV_LEFT + k, tl), :] * w_ref[k:k + 1, :]
    return y


def _lru_kernel(dm, tl, reverse, *refs):
    if reverse:
        (xa_ref, xp_ref, xn_ref, cw_ref, cb_ref, wr_ref, br_ref, wi_ref, bi_ref, lam_ref, h0_ref,
         ga_ref, hf_ref, y_ref, hT_ref, ext_ref, a_ref, bx_ref, hs_ref, h_ref) = refs
    else:
        (xa_ref, xp_ref, xn_ref, cw_ref, cb_ref, wr_ref, br_ref, wi_ref, bi_ref, lam_ref, h0_ref,
         y_ref, hT_ref, ext_ref, a_ref, bx_ref, hs_ref, h_ref) = refs
    nt = dm.T // tl
    i = pl.program_id(0)
    tile = nt - 1 - i if reverse else i
    pos, lseq, _ = _seq_tile_info(dm, tile, tl)
    is_first = pos == 0
    is_last = pos + tl == lseq
    _fill_ext(ext_ref, xa_ref, xp_ref, xn_ref, is_first, is_last, tl)
    xc = _conv_from_ext(ext_ref, cw_ref, cb_ref, tl)
    sp = _softplus(-lam_ref[...])
    bw = dm.lru_bw
    for n in range(dm.lru_blocks):
        cs = slice(n * bw, (n + 1) * bw)
        xb = xc[:, cs]
        xbh = xb.astype(BF16)
        r = _sigmoid(jnp.dot(xbh, wr_ref[n], preferred_element_type=F32) + br_ref[:, cs])
        ig = _sigmoid(jnp.dot(xbh, wi_ref[n], preferred_element_type=F32) + bi_ref[:, cs])
        log_a = (-LRU_C) * r * sp[:, cs]
        a = jnp.exp(log_a)
        a_ref[:, cs] = a
        bx_ref[:, cs] = jnp.sqrt(1.0 - a * a) * (ig * xb)

    start_of_scan = is_last if reverse else is_first

    @pl.when(start_of_scan)
    def _():
        h_ref[...] = h0_ref[...]

    def body(t, h):
        tt = tl - 1 - t if reverse else t
        h = a_ref[pl.ds(tt, 1), :] * h + bx_ref[pl.ds(tt, 1), :]
        hs_ref[pl.ds(tt, 1), :] = h
        return h

    h = lax.fori_loop(0, tl, body, h_ref[...], unroll=8)
    h_ref[...] = h
    hT_ref[...] = h
    if reverse:
        y = (hf_ref[...].astype(F32) + hs_ref[...]) * _gelu_tanh(ga_ref[...].astype(F32))
        y_ref[...] = y.astype(y_ref.dtype)
    else:
        y_ref[...] = hs_ref[...].astype(y_ref.dtype)


def _lru(dm, reverse, proj, cw, cb, wr, br, wi, bi, lam, h0, hs_fwd=None, tl=256):
    nt = dm.T // tl
    c = dm.d_lru
    tile_of = (lambda i: nt - 1 - i) if reverse else (lambda i: i)
    seq_of = lambda i: _seq_tile_info(dm, tile_of(i), tl)[2]
    full = lambda shape: pl.BlockSpec(shape, lambda i: (0,) * len(shape))
    row = pl.BlockSpec((tl, c), lambda i: (tile_of(i), 0))
    in_specs = _halo_specs(dm, tl, c, dm.o_xa // c, tile_of) + [
        full((CONV_W, c)), full((1, c)),
        full(wr.shape), full((1, c)), full(wi.shape), full((1, c)), full((1, c)),
        pl.BlockSpec((None, 1, c), lambda i: (seq_of(i), 0, 0)),
    ]
    args = [proj, proj, proj, cw, cb, wr, br, wi, bi, lam, h0]
    if reverse:
        in_specs += [pl.BlockSpec((tl, c), lambda i: (tile_of(i), dm.o_ga // c)), row]
        args += [proj, hs_fwd]
    return pl.pallas_call(
        functools.partial(_lru_kernel, dm, tl, reverse),
        grid=(nt,),
        in_specs=in_specs,
        out_specs=[row, pl.BlockSpec((None, 1, c), lambda i: (seq_of(i), 0, 0))],
        out_shape=[jax.ShapeDtypeStruct((dm.T, c), BF16), jax.ShapeDtypeStruct((dm.nseq, 1, c), F32)],
        scratch_shapes=[pltpu.VMEM((tl + 2 * HALO, c), F32), pltpu.VMEM((tl, c), F32), pltpu.VMEM((tl, c), F32),
                        pltpu.VMEM((tl, c), F32), pltpu.VMEM((1, c), F32)],
        compiler_params=_cparams(("arbitrary",)),
        name="lru_bwd" if reverse else "lru_fwd",
    )(*args)


def _pool_kernel(dm, tl, xc_ref, xp_ref, xn_ref, w_ref, s_ref, o_ref, ext_ref):
    tile = pl.program_id(0)
    pos, lseq, _ = _seq_tile_info(dm, tile, tl)
    _fill_ext(ext_ref, xc_ref, xp_ref, xn_ref, pos == 0, pos + tl == lseq, tl)
    t = pos + lax.broadcasted_iota(jnp.int32, (tl, 1), 0)
    pg = dm.pool_group
    for k, w in enumerate(POOL_WINDOWS):
        cs = slice(k * pg, (k + 1) * pg)
        tok_rows = ext_ref[pl.ds(HALO, tl), cs]
        acc = tok_rows
        for o in range(-w // 2, w // 2):
            if o != 0:
                acc = acc + ext_ref[pl.ds(HALO + o, tl), cs]
        cnt = (jnp.minimum(t + w // 2, lseq) - jnp.maximum(t - w // 2, 0)).astype(F32)
        dev = acc / cnt - tok_rows
        y = jnp.dot(dev.astype(BF16), w_ref[k], preferred_element_type=F32)
        o_ref[:, cs] = (y * s_ref[:, cs]).astype(o_ref.dtype)


def _pool(dm, proj, w_pool, scale, tl=256):
    c = dm.d_pool
    return pl.pallas_call(
        functools.partial(_pool_kernel, dm, tl),
        grid=(dm.T // tl,),
        in_specs=_halo_specs(dm, tl, c, dm.o_xc // c, lambda i: i) + [
            pl.BlockSpec(w_pool.shape, lambda i: (0, 0, 0)), pl.BlockSpec((1, c), lambda i: (0, 0))],
        out_specs=pl.BlockSpec((tl, c), lambda i: (i, 0)),
        out_shape=jax.ShapeDtypeStruct((dm.T, c), BF16),
        scratch_shapes=[pltpu.VMEM((tl + 2 * HALO, c), F32)],
        compiler_params=_cparams(("arbitrary",)),
        name="pool",
    )(proj, proj, proj, w_pool, scale)


def _dot_exact(a, b):
    return jnp.dot(a, b, preferred_element_type=F32, precision=lax.Precision.HIGHEST)


def _ssd_kernel(dm, q, reverse, *refs):
    if reverse:
        (xact_ref, dt_ref, dtb_ref, alog_ref, h0_ref, z_ref, yf_ref, ng_ref,
         y_ref, hT_ref, h_ref, yacc_ref) = refs
    else:
        (xs_ref, xsp_ref, xsn_ref, bc_ref, bcp_ref, bcn_ref, dt_ref, cwx_ref, cbx_ref, cwb_ref, cbb_ref,
         dtb_ref, alog_ref, h0_ref, dsk_ref, y_ref, hT_ref, xact_ref, extx_ref, extb_ref, h_ref) = refs
    nt = dm.T // q
    i = pl.program_id(0)
    tile = nt - 1 - i if reverse else i
    pos, lseq, _ = _seq_tile_info(dm, tile, q)
    is_first = pos == 0
    is_last = pos + q == lseq
    d_ssd, n, g_cnt, r_cnt, p = dm.d_ssd, dm.N, dm.G, dm.R, dm.P
    gn = g_cnt * n
    pair = 2 * p
    assert pair == LANES and n == LANES and q == LANES
    if reverse:
        xs = xact_ref[:, :d_ssd].astype(F32)
        bc = xact_ref[:, d_ssd:].astype(F32)
    else:
        _fill_ext(extx_ref, xs_ref, xsp_ref, xsn_ref, is_first, is_last, q)
        _fill_ext(extb_ref, bc_ref, bcp_ref, bcn_ref, is_first, is_last, q)
        xs = _silu(_conv_from_ext(extx_ref, cwx_ref, cbx_ref, q))
        bc = _silu(_conv_from_ext(extb_ref, cwb_ref, cbb_ref, q))
        xact_ref[:, :d_ssd] = xs.astype(xact_ref.dtype)
        xact_ref[:, d_ssd:] = bc.astype(xact_ref.dtype)

    @pl.when(is_last if reverse else is_first)
    def _():
        h_ref[...] = h0_ref[...]

    dt = _softplus(dt_ref[...] + dtb_ref[...])
    a_neg = -jnp.exp(alog_ref[...])
    dta = dt * a_neg
    ri = lax.broadcasted_iota(jnp.int32, (q, q), 0)
    ci = lax.broadcasted_iota(jnp.int32, (q, q), 1)
    valid = (ci >= ri) if reverse else (ci <= ri)
    a_cs = _dot_exact(valid.astype(F32), dta)
    a_end = a_cs[0:1, :] if reverse else a_cs[q - 1:q, :]
    w_state = dt * jnp.exp(a_end - a_cs)
    src_t = (a_cs - jnp.log(dt)).T
    cd_t = jnp.broadcast_to(jnp.exp(a_end), (q, LANES)).T
    lane = lax.broadcasted_iota(jnp.int32, (q, LANES), 1)
    low_half = lane < p

    for g in range(g_cnt):
        b_h = bc[:, g * n:(g + 1) * n].astype(BF16)
        c_h = bc[:, gn + g * n:gn + (g + 1) * n].astype(BF16)
        scores = lax.dot_general(c_h, b_h, (((1,), (1,)), ((), ())), preferred_element_type=F32)
        r0 = g * r_cnt * p
        h_grp = h_ref[pl.ds(r0, r_cnt * p), :].astype(BF16)
        y_off = lax.dot_general(c_h, h_grp, (((1,), (1,)), ((), ())), preferred_element_type=F32)
        for k in range(r_cnt // 2):
            c0 = r0 + 2 * k * p
            hd0 = g * r_cnt + 2 * k
            xs_pair = xs[:, c0:c0 + pair]
            xs_h = xs_pair.astype(BF16)
            ys, eas, ws = [], [], []
            for s in range(2):
                hd = hd0 + s
                col = jnp.broadcast_to(a_cs[:, hd:hd + 1], (q, q))
                row = jnp.broadcast_to(src_t[hd:hd + 1, :], (q, q))
                m = (scores * jnp.exp(jnp.where(valid, col - row, -jnp.inf))).astype(BF16)
                ys.append(jnp.dot(m, xs_h, preferred_element_type=F32))
                eas.append(jnp.exp(col))
                ws.append(jnp.broadcast_to(w_state[:, hd:hd + 1], (q, LANES)))
            y_pair = (jnp.where(low_half, ys[0], ys[1])
                      + y_off[:, c0 - r0:c0 - r0 + pair] * jnp.where(low_half, eas[0], eas[1]))
            xsw = (xs_pair * jnp.where(low_half, ws[0], ws[1])).astype(BF16)
            st = lax.dot_general(xsw, b_h, (((0,), (0,)), ((), ())), preferred_element_type=F32)
            dec = jnp.concatenate([jnp.broadcast_to(cd_t[hd0:hd0 + 1, :], (p, n)),
                                   jnp.broadcast_to(cd_t[hd0 + 1:hd0 + 2, :], (p, n))], axis=0)
            h_ref[pl.ds(c0, pair), :] = dec * h_ref[pl.ds(c0, pair), :] + st
            if reverse:
                yacc_ref[:, c0:c0 + pair] = y_pair
            else:
                y_ref[:, c0:c0 + pair] = (y_pair + dsk_ref[:, c0:c0 + pair] * xs_pair).astype(y_ref.dtype)

    hT_ref[...] = h_ref[...]
    if reverse:
        gw = d_ssd // g_cnt
        for g in range(g_cnt):
            cs = slice(g * gw, (g + 1) * gw)
            y = (yacc_ref[:, cs] + yf_ref[:, cs]) * _silu(z_ref[:, cs].astype(F32))
            ms = jnp.mean(y * y, axis=-1, keepdims=True)
            y_ref[:, cs] = (y * lax.rsqrt(ms + EPS) * ng_ref[:, cs]).astype(y_ref.dtype)


def _ssd(dm, reverse, proj, dtp, cw, cb, dtb, alog, h0, extra, fwd=None, q=128):
    nt = dm.T // q
    tile_of = (lambda i: nt - 1 - i) if reverse else (lambda i: i)
    seq_of = lambda i: _seq_tile_info(dm, tile_of(i), q)[2]
    full = lambda shape: pl.BlockSpec(shape, lambda i: (0,) * len(shape))
    hp = dm.H * dm.P
    rowy = pl.BlockSpec((q, dm.d_ssd), lambda i: (tile_of(i), 0))
    rowact = pl.BlockSpec((q, dm.d_xbc), lambda i: (tile_of(i), 0))
    state = pl.BlockSpec((None, hp, dm.N), lambda i: (seq_of(i), 0, 0))
    dt_spec = pl.BlockSpec((q, LANES), lambda i: (tile_of(i), 1 if reverse else 0))
    vec = [full((1, LANES)), full((1, LANES)), state]
    if reverse:
        y_fwd, xact = fwd
        in_specs = [rowact, dt_spec] + vec + [
            pl.BlockSpec((q, dm.d_ssd), lambda i: (tile_of(i), dm.o_z // dm.d_ssd)), rowy, full((1, dm.d_ssd))]
        args = [xact, dtp, dtb, alog, h0, proj, y_fwd, extra]
        out_specs = [rowy, state]
        out_shape = [jax.ShapeDtypeStruct((dm.T, dm.d_ssd), BF16), jax.ShapeDtypeStruct((dm.nseq, hp, dm.N), F32)]
        scratch = [pltpu.VMEM((hp, dm.N), F32), pltpu.VMEM((q, dm.d_ssd), F32)]
    else:
        in_specs = (_halo_specs(dm, q, dm.d_ssd, dm.o_xs // dm.d_ssd, tile_of)
                    + _halo_specs(dm, q, dm.d_bc, dm.o_bc // dm.d_bc, tile_of)
                    + [dt_spec, full((CONV_W, dm.d_ssd)), full((1, dm.d_ssd)), full((CONV_W, dm.d_bc)),
                       full((1, dm.d_bc))] + vec + [full((1, dm.d_ssd))])
        args = [proj] * 6 + [dtp, cw[:, :dm.d_ssd], cb[:, :dm.d_ssd], cw[:, dm.d_ssd:], cb[:, dm.d_ssd:],
                             dtb, alog, h0, extra]
        out_specs = [rowy, state, rowact]
        out_shape = [jax.ShapeDtypeStruct((dm.T, dm.d_ssd), F32), jax.ShapeDtypeStruct((dm.nseq, hp, dm.N), F32),
                     jax.ShapeDtypeStruct((dm.T, dm.d_xbc), BF16)]
        scratch = [pltpu.VMEM((q + 2 * HALO, dm.d_ssd), F32), pltpu.VMEM((q + 2 * HALO, dm.d_bc), F32),
                   pltpu.VMEM((hp, dm.N), F32)]
    return pl.pallas_call(
        functools.partial(_ssd_kernel, dm, q, reverse),
        grid=(nt,),
        in_specs=in_specs,
        out_specs=out_specs,
        out_shape=out_shape,
        scratch_shapes=scratch,
        compiler_params=_cparams(("arbitrary",)),
        name="ssd_bwd" if reverse else "ssd_fwd",
    )(*args)


def _moe_gather_kernel(tm, tok_ref, src_hbm, o_ref, buf, sem):
    i = pl.program_id(0)
    n = pl.num_programs(0)
    slot = lax.rem(i, 2)

    @pl.when(i == 0)
    def _():
        _row_gather_start(src_hbm, tok_ref, 0, buf.at[0], sem.at[0], tm, 0)

    @pl.when(i + 1 < n)
    def _():
        _row_gather_start(src_hbm, tok_ref, (i + 1) * tm, buf.at[1 - slot], sem.at[1 - slot], tm, 0)

    _row_gather_wait(src_hbm, buf.at[slot], sem.at[slot], tm)
    o_ref[...] = buf[slot]


def _moe_gather(src, tok, tm):
    p_pad = tok.shape[0]
    words = src.shape[1]
    return pl.pallas_call(
        functools.partial(_moe_gather_kernel, tm),
        grid_spec=pltpu.PrefetchScalarGridSpec(
            num_scalar_prefetch=1, grid=(p_pad // tm,),
            in_specs=[pl.BlockSpec(memory_space=pl.ANY)],
            out_specs=pl.BlockSpec((tm, words), lambda i, tok_ref: (i, 0)),
            scratch_shapes=[pltpu.VMEM((2, tm, words), jnp.uint32), pltpu.SemaphoreType.DMA((2,))]),
        out_shape=jax.ShapeDtypeStruct((p_pad, words), jnp.uint32),
        compiler_params=_gather_cparams(),
        name="moe_gather",
    )(tok, src)


def _expert_changed(te_ref, i):
    return jnp.logical_or(i == 0, te_ref[i] != te_ref[jnp.maximum(i - 1, 0)])


def _expert_up_kernel(te_ref, nu_ref, x_ref, wg_ref, wu_ref, h_ref, wg_s, wu_s):
    i = pl.program_id(1)

    @pl.when(jnp.logical_and(i < nu_ref[0], _expert_changed(te_ref, i)))
    def _():
        wg_s[...] = wg_ref[...].astype(BF16)
        wu_s[...] = wu_ref[...].astype(BF16)

    @pl.when(i < nu_ref[0])
    def _():
        lo, hi = _unpack_halves(x_ref[...])
        lo = lo.astype(BF16)
        hi = hi.astype(BF16)
        half = lo.shape[1]
        a = (jnp.dot(lo, wg_s[:half, :], preferred_element_type=F32)
             + jnp.dot(hi, wg_s[half:, :], preferred_element_type=F32))
        b = (jnp.dot(lo, wu_s[:half, :], preferred_element_type=F32)
             + jnp.dot(hi, wu_s[half:, :], preferred_element_type=F32))
        h_ref[...] = (_silu(a) * b).astype(h_ref.dtype)

    @pl.when(i >= nu_ref[0])
    def _():
        h_ref[...] = jnp.zeros_like(h_ref)


def _expert_down_kernel(te_ref, nu_ref, h_ref, wd_ref, y_ref, wd_s):
    i = pl.program_id(1)

    @pl.when(jnp.logical_and(i < nu_ref[0], _expert_changed(te_ref, i)))
    def _():
        wd_s[...] = wd_ref[...].astype(BF16)

    @pl.when(i < nu_ref[0])
    def _():
        y_ref[...] = _pack_halves(jnp.dot(h_ref[...], wd_s[...], preferred_element_type=F32))

    @pl.when(i >= nu_ref[0])
    def _():
        y_ref[...] = jnp.zeros_like(y_ref)


def _experts(dm, layer, xsp, w_gate, w_up, w_down, tile_expert, n_used, tm, n_split):
    p_pad = xsp.shape[0]
    n_tiles = p_pad // tm
    d, f = dm.D, dm.F
    fh, dh = f // n_split, d // n_split
    h = pl.pallas_call(
        _expert_up_kernel,
        grid_spec=pltpu.PrefetchScalarGridSpec(
            num_scalar_prefetch=2, grid=(n_split, n_tiles),
            in_specs=[pl.BlockSpec((tm, d // 2), lambda j, i, te, nu: (i, 0)),
                      pl.BlockSpec((None, None, d, fh), lambda j, i, te, nu: (layer, te[i], 0, j)),
                      pl.BlockSpec((None, None, d, fh), lambda j, i, te, nu: (layer, te[i], 0, j))],
            out_specs=pl.BlockSpec((tm, fh), lambda j, i, te, nu: (i, j)),
            scratch_shapes=[pltpu.VMEM((d, fh), BF16), pltpu.VMEM((d, fh), BF16)]),
        out_shape=jax.ShapeDtypeStruct((p_pad, f), BF16),
        compiler_params=_cparams(("arbitrary", "arbitrary")),
        name="expert_up",
    )(tile_expert, n_used, xsp, w_gate, w_up)
    return pl.pallas_call(
        _expert_down_kernel,
        grid_spec=pltpu.PrefetchScalarGridSpec(
            num_scalar_prefetch=2, grid=(n_split, n_tiles),
            in_specs=[pl.BlockSpec((tm, f), lambda j, i, te, nu: (i, 0)),
                      pl.BlockSpec((None, None, f, dh), lambda j, i, te, nu: (layer, te[i], 0, j))],
            out_specs=pl.BlockSpec((tm, dh // 2), lambda j, i, te, nu: (i, j)),
            scratch_shapes=[pltpu.VMEM((f, dh), BF16)]),
        out_shape=jax.ShapeDtypeStruct((p_pad, d // 2), jnp.uint32),
        compiler_params=_cparams(("arbitrary", "arbitrary")),
        name="expert_down",
    )(tile_expert, n_used, h, w_down)


def _dispatch_plan(dm, route, tm):
    t = dm.T
    e_cnt = dm.E
    e_idx = route[0:2].astype(jnp.int32).reshape(-1)
    onehot = (e_idx[:, None] == jnp.arange(e_cnt, dtype=jnp.int32)[None, :]).astype(jnp.int32)
    rank = jnp.sum((jnp.cumsum(onehot, axis=0) - onehot) * onehot, axis=1)
    counts = jnp.sum(onehot, axis=0)
    tiles_per = (counts + tm - 1) // tm
    tile_end = jnp.cumsum(tiles_per)
    start = (tile_end - tiles_per) * tm
    slot = jnp.sum(onehot * start[None, :], axis=1) + rank
    n_tiles = (TOP_K * t) // tm + e_cnt
    p_pad = n_tiles * tm
    tok = jnp.zeros((p_pad,), jnp.int32).at[slot].set(jnp.tile(jnp.arange(t, dtype=jnp.int32), TOP_K),
                                                      unique_indices=True, mode="promise_in_bounds")
    tile_ids = jnp.arange(n_tiles, dtype=jnp.int32)
    tile_expert = jnp.minimum(jnp.sum((tile_end[None, :] <= tile_ids[:, None]).astype(jnp.int32), axis=1),
                              e_cnt - 1)
    n_used = tile_end[-1:].astype(jnp.int32)
    return tok, tile_expert, n_used, slot[:t], slot[t:]


def _grid_pos_embed(n_tokens, d_model):
    rows = n_tokens // GRID_W
    row = jnp.repeat(jnp.arange(rows), GRID_W).astype(F32)
    col = jnp.tile(jnp.arange(GRID_W), rows).astype(F32)
    quarter = d_model // 4
    omega = 1.0 / (10000.0 ** (jnp.arange(quarter, dtype=F32) / quarter))
    ang_r = row[:, None] * omega[None, :]
    ang_c = col[:, None] * omega[None, :]
    return jnp.concatenate([jnp.sin(ang_r), jnp.cos(ang_r), jnp.sin(ang_c), jnp.cos(ang_c)], axis=-1)


def _pad_lanes(v, fill=0.0):
    return jnp.pad(v.astype(F32), (0, LANES - v.shape[0]), constant_values=fill).reshape(1, LANES)


def kernel(x_prompt, x_sample, state_lru, state_ssd, c, c_ctx, w_mod, b_mod, norm1_g, w_in, conv_a_w, conv_a_b, w_r, b_r, w_i, b_i, lru_lambda, w_oa, conv_b_w, conv_b_b, dt_bias, a_log, d_skip, ssd_norm_g, w_ob, w_pool, pool_scale, w_oc, w_out, norm2_g, w_router, b_router, w_gate, w_up, w_down, final_g):
    dm = Dims(x_prompt, x_sample, state_ssd, w_r, w_pool, w_gate, dt_bias, conv_b_w, w_oa, w_ob, w_oc)
    d = dm.D
    moe_tm = min(512, dm.Lp)

    ncond = 16
    cond = jnp.zeros((ncond, d), F32).at[:dm.Bs].set(c).at[dm.Bs].set(c_ctx)
    mod4 = _mod_table(cond, w_mod, b_mod).reshape(dm.depth, ncond, 1, 6 * d)

    pos = _grid_pos_embed(dm.Ls, d)
    x, u = _embed_norm(dm, x_sample.reshape(dm.Ts, d), x_prompt.reshape(dm.Tp, d), pos,
                       norm1_g[0].reshape(1, d), mod4)

    wr_pad = jnp.pad(w_router, ((0, 0), (0, LANES - dm.E)))
    br_col = b_router.reshape(dm.E, 1).astype(F32)
    hp = dm.H * dm.P
    lru_states, ssd_states = [], []
    out = None
    for l in range(dm.depth):
        wl = w_in[l]
        w_main = jnp.concatenate([wl[:, :dm.o_dt], wl[:, dm.o_dt + 2 * dm.H:]], axis=1).astype(BF16)
        lane_pad = ((0, 0), (0, LANES - dm.H))
        w_dt = jnp.concatenate([jnp.pad(wl[:, dm.o_dt:dm.o_dt + dm.H], lane_pad),
                                jnp.pad(wl[:, dm.o_dt + dm.H:dm.o_dt + 2 * dm.H], lane_pad)], axis=1).astype(BF16)
        proj = _matmul(u, w_main, BF16, min(1024, dm.Lp), min(1024, dm.d_lru), "in_proj")
        dtp = _matmul(u, w_dt, F32, min(512, dm.Lp), 2 * LANES, "dt_proj")

        h0_lru = [jnp.concatenate([state_lru[:, l, dd], jnp.zeros((dm.Bp, dm.d_lru), F32)], axis=0)
                  .reshape(dm.nseq, 1, dm.d_lru) for dd in range(2)]
        lru_args = lambda dd: (conv_a_w[l], conv_a_b[l].reshape(1, -1), w_r[l, dd].astype(BF16),
                               b_r[l, dd].reshape(1, -1), w_i[l, dd].astype(BF16), b_i[l, dd].reshape(1, -1),
                               lru_lambda[l, dd].reshape(1, -1), h0_lru[dd])
        hs_f, lru_tf = _lru(dm, False, proj, *lru_args(0))
        ya, lru_tb = _lru(dm, True, proj, *lru_args(1), hs_fwd=hs_f)
        lru_states.append(jnp.stack([lru_tf[dm.Bs:, 0], lru_tb[dm.Bs:, 0]], axis=1))

        h0_ssd = [jnp.concatenate([state_ssd[:, l, dd].reshape(dm.Bs, hp, dm.N),
                                   jnp.zeros((dm.Bp, hp, dm.N), F32)], axis=0) for dd in range(2)]
        ssd_args = lambda dd: (conv_b_w[l], conv_b_b[l].reshape(1, -1), _pad_lanes(dt_bias[l, dd]),
                               _pad_lanes(a_log[l, dd]), h0_ssd[dd])
        dsk = jnp.repeat(d_skip[l].astype(F32), dm.P).reshape(1, dm.d_ssd)
        y_f, ssd_tf, xact = _ssd(dm, False, proj, dtp, *ssd_args(0), dsk)
        yb, ssd_tb = _ssd(dm, True, proj, dtp, *ssd_args(1), ssd_norm_g[l].reshape(1, -1), fwd=(y_f, xact))
        ssd_states.append(jnp.stack([ssd_tf[dm.Bs:], ssd_tb[dm.Bs:]], axis=1)
                          .reshape(dm.Bp, 2, dm.H, dm.P, dm.N))

        yc = _pool(dm, proj, w_pool[l].astype(BF16), pool_scale[l].reshape(1, -1))

        merged = _merge(dm, ya, yb, yc, proj, w_oa[l].astype(BF16), w_ob[l].astype(BF16),
                        w_oc[l].astype(BF16), tm=min(512, dm.Lp))
        x = _outproj(dm, l, merged, w_out[l].astype(BF16), x, mod4, tm=min(512, dm.Lp))

        u2p, route = _norm_router(dm, l, x, norm2_g[l].reshape(1, d), mod4, wr_pad, br_col)
        tok, tile_expert, n_used, slot1, slot2 = _dispatch_plan(dm, route, moe_tm)
        n_split = 2
        ysp = _experts(dm, l, _moe_gather(u2p, tok, moe_tm), w_gate, w_up, w_down, tile_expert, n_used, moe_tm,
                       n_split)
        w1 = route[2].reshape(dm.T, 1)
        w2 = route[3].reshape(dm.T, 1)
        if l + 1 < dm.depth:
            x, u = _combine_norm(dm, l, x, ysp, slot1, slot2, w1, w2, norm1_g[l + 1].reshape(1, d), mod4, n_split)
        else:
            fg = final_g.reshape(1, d)
            out = (_combine_final(dm, l, x, ysp, slot1, slot2, w1, w2, fg, mod4, n_split, 0, dm.Ts),
                   _combine_final(dm, l, x, ysp, slot1, slot2, w1, w2, fg, mod4, n_split, dm.Ts, dm.Tp))

    y_sample = out[0].reshape(dm.Bs, dm.Ls, d)
    y_prompt = out[1].reshape(dm.Bp, dm.Lp, d)
    new_state_lru = jnp.stack(lru_states, axis=1).astype(x_prompt.dtype)
    new_state_ssd = jnp.stack(ssd_states, axis=1).astype(x_prompt.dtype)
    return (y_prompt, y_sample, new_state_lru, new_state_ssd)
```

```python
import functools
import math

import numpy as np
import jax
import jax.numpy as jnp
from jax import lax
from jax.experimental import pallas as pl
from jax.experimental.pallas import tpu as pltpu

F32 = jnp.float32
BF16 = jnp.bfloat16

EPS = 1e-6
GRID_W = 64
CONV_W = 4
CONV_LEFT = 1
LRU_C = 8.0
POOL_WINDOWS = (2, 4, 8, 16)
N_EXPERT_GROUPS = 4
TOP_K = 2

LANES = 128
HALO = 16
VMEM_LIMIT = 56 * 1024 * 1024


def _cparams(sem):
    return pltpu.CompilerParams(dimension_semantics=sem, vmem_limit_bytes=VMEM_LIMIT)


def _sigmoid_exp(x):
    return 1.0 / (1.0 + jnp.exp(-x))


def _sigmoid(x):
    return 0.5 * jnp.tanh(0.5 * x) + 0.5


def _silu(x):
    return x * _sigmoid(x)


def _softplus(x):
    return jnp.maximum(x, 0.0) + jnp.log(1.0 + jnp.exp(-jnp.abs(x)))


def _gelu_tanh(x):
    return x * (0.5 * (1.0 + jnp.tanh(math.sqrt(2.0 / math.pi) * (x + 0.044715 * (x * x * x)))))


def _pack_halves(x):
    c = x.shape[1] // 2
    lo = lax.bitcast_convert_type(x[:, :c].astype(BF16).astype(F32), jnp.uint32)
    hi = lax.bitcast_convert_type(x[:, c:].astype(BF16).astype(F32), jnp.uint32)
    return (hi & jnp.uint32(0xFFFF0000)) | (lo >> 16)


def _unpack_halves(w):
    lo = lax.bitcast_convert_type(w << 16, F32)
    hi = lax.bitcast_convert_type(w & jnp.uint32(0xFFFF0000), F32)
    return lo, hi


class Dims:
    def __init__(self, x_prompt, x_sample, state_ssd, w_r, w_pool, w_gate, dt_bias, conv_b_w, w_oa, w_ob, w_oc):
        self.Bp, self.Lp, self.D = x_prompt.shape
        self.Bs, self.Ls, _ = x_sample.shape
        self.depth = w_r.shape[0]
        self.Ts = self.Bs * self.Ls
        self.Tp = self.Bp * self.Lp
        self.T = self.Ts + self.Tp
        self.nseq = self.Bs + self.Bp
        self.d_lru = w_oa.shape[1]
        self.lru_blocks = w_r.shape[2]
        self.lru_bw = w_r.shape[3]
        self.d_ssd = w_ob.shape[1]
        self.H = dt_bias.shape[2]
        self.P = self.d_ssd // self.H
        self.N = state_ssd.shape[-1]
        self.d_xbc = conv_b_w.shape[2]
        self.G = (self.d_xbc - self.d_ssd) // (2 * self.N)
        self.R = self.H // self.G
        self.d_pool = w_oc.shape[1]
        self.pool_group = w_pool.shape[2]
        self.E = w_gate.shape[1]
        self.F = w_gate.shape[3]
        self.d_bc = 2 * self.G * self.N
        self.o_xa = 0
        self.o_ga = self.o_xa + self.d_lru
        self.o_z = self.o_ga + self.d_lru
        self.o_xs = self.o_z + self.d_ssd
        self.o_bc = self.o_xs + self.d_ssd
        self.o_dt = self.o_bc + self.d_bc
        self.o_xc = self.o_bc + self.d_bc
        self.o_gate = self.o_xc + self.d_pool
        self.n_main = self.o_gate + 3 * self.D


def _seq_tile_info(dm, tile, tl):
    row0 = tile * tl
    in_sample = row0 < dm.Ts
    rp = jnp.maximum(row0 - dm.Ts, 0)
    pos = jnp.where(in_sample, row0 % dm.Ls, rp % dm.Lp)
    lseq = jnp.where(in_sample, dm.Ls, dm.Lp)
    seq = jnp.where(in_sample, row0 // dm.Ls, dm.Bs + rp // dm.Lp)
    return pos, lseq, seq


def _cond_idx(dm, i, tm):
    return jnp.minimum((i * tm) // dm.Ls, dm.Bs)


def _mod_kernel(c_ref, w_ref, b_ref, o_ref):
    c = c_ref[...]
    a = _silu(c).astype(BF16)
    o_ref[...] = jnp.dot(a, w_ref[...].astype(BF16), preferred_element_type=F32) + b_ref[...]


def _mod_table(cond, w_mod, b_mod, tn=512):
    depth, d, n6 = w_mod.shape
    nc = cond.shape[0]
    return pl.pallas_call(
        _mod_kernel,
        grid=(depth, n6 // tn),
        in_specs=[
            pl.BlockSpec((nc, d), lambda l, j: (0, 0)),
            pl.BlockSpec((None, d, tn), lambda l, j: (l, 0, j)),
            pl.BlockSpec((None, 1, tn), lambda l, j: (l, 0, j)),
        ],
        out_specs=pl.BlockSpec((None, nc, tn), lambda l, j: (l, 0, j)),
        out_shape=jax.ShapeDtypeStruct((depth, nc, n6), F32),
        compiler_params=_cparams(("arbitrary", "arbitrary")),
        name="mod_table",
    )(cond, w_mod, b_mod.reshape(depth, 1, n6))


def _norm_mod(x, g, sc, sh):
    ms = jnp.mean(x * x, axis=-1, keepdims=True)
    y = x * lax.rsqrt(ms + EPS) * g
    return y * (1.0 + sc) + sh


def _mod_spec(dm, layer, section, tm):
    return pl.BlockSpec((None, None, 1, dm.D), lambda i: (layer, _cond_idx(dm, i, tm), 0, section))


def _embed_norm_kernel(dm, tm, xs_ref, xp_ref, pos_ref, g_ref, sc_ref, sh_ref, x_ref, u_ref):
    i = pl.program_id(0)
    n_s = dm.Ts // tm

    @pl.when(i < n_s)
    def _():
        x_ref[...] = xs_ref[...] + pos_ref[...]

    @pl.when(i >= n_s)
    def _():
        x_ref[...] = xp_ref[...]

    u_ref[...] = _norm_mod(x_ref[...], g_ref[...], sc_ref[...], sh_ref[...]).astype(u_ref.dtype)


def _embed_norm(dm, xs2, xp2, pos, g, mod4, tm=256):
    n_s = dm.Ts // tm
    n_l = dm.Ls // tm
    row = pl.BlockSpec((tm, dm.D), lambda i: (i, 0))
    return pl.pallas_call(
        functools.partial(_embed_norm_kernel, dm, tm),
        grid=(dm.T // tm,),
        in_specs=[
            pl.BlockSpec((tm, dm.D), lambda i: (jnp.minimum(i, n_s - 1), 0)),
            pl.BlockSpec((tm, dm.D), lambda i: (jnp.maximum(i - n_s, 0), 0)),
            pl.BlockSpec((tm, dm.D), lambda i: (i % n_l, 0)),
            pl.BlockSpec((1, dm.D), lambda i: (0, 0)),
            _mod_spec(dm, 0, 1, tm),
            _mod_spec(dm, 0, 0, tm),
        ],
        out_specs=[row, row],
        out_shape=[jax.ShapeDtypeStruct((dm.T, dm.D), F32), jax.ShapeDtypeStruct((dm.T, dm.D), BF16)],
        compiler_params=_cparams(("arbitrary",)),
        name="embed_norm",
    )(xs2, xp2, pos, g, mod4, mod4)


def _route_rows(s, sel):
    e_total = s.shape[0]
    per = e_total // N_EXPERT_GROUPS
    assert per == 4 and TOP_K == 2
    srow = [s[e:e + 1, :] for e in range(e_total)]
    vrow = [sel[e:e + 1, :] for e in range(e_total)]
    scores = []
    for gi in range(N_EXPERT_GROUPS):
        a, b, c, d = vrow[4 * gi:4 * gi + 4]
        hi1, lo1 = jnp.maximum(a, b), jnp.minimum(a, b)
        hi2, lo2 = jnp.maximum(c, d), jnp.minimum(c, d)
        top = jnp.maximum(hi1, hi2)
        second = jnp.maximum(jnp.minimum(hi1, hi2), jnp.maximum(lo1, lo2))
        scores.append(top + second)
    g = jnp.zeros_like(scores[0])
    best = scores[0]
    for gi in range(1, N_EXPERT_GROUPS):
        upd = scores[gi] > best
        g = jnp.where(upd, float(gi), g)
        best = jnp.where(upd, scores[gi], best)

    def pick(rows, j):
        out = rows[j]
        for gi in range(1, N_EXPERT_GROUPS):
            out = jnp.where(g == float(gi), rows[4 * gi + j], out)
        return out

    v = [pick(vrow, j) for j in range(4)]
    sv = [pick(srow, j) for j in range(4)]
    i1 = jnp.zeros_like(g)
    b1 = v[0]
    for j in range(1, 4):
        upd = v[j] > b1
        i1 = jnp.where(upd, float(j), i1)
        b1 = jnp.where(upd, v[j], b1)
    i2 = jnp.zeros_like(g)
    b2 = jnp.full_like(g, -jnp.inf)
    for j in range(4):
        upd = jnp.logical_and(i1 != float(j), v[j] > b2)
        i2 = jnp.where(upd, float(j), i2)
        b2 = jnp.where(upd, v[j], b2)

    def pick_idx(idx):
        out = sv[0]
        for j in range(1, 4):
            out = jnp.where(idx == float(j), sv[j], out)
        return out

    s1, s2 = pick_idx(i1), pick_idx(i2)
    tot = s1 + s2
    return g * 4.0 + i1, g * 4.0 + i2, s1 / tot, s2 / tot


def _norm_router_kernel(x_ref, g_ref, sc_ref, sh_ref, wr_ref, br_ref, u_ref, r_ref):
    u = _norm_mod(x_ref[...], g_ref[...], sc_ref[...], sh_ref[...])
    u_hi = u.astype(BF16)
    u_ref[...] = _pack_halves(u)
    u_lo = (u - u_hi.astype(F32)).astype(BF16)
    w = wr_ref[...]
    w_hi = w.astype(BF16)
    w_lo = (w - w_hi.astype(F32)).astype(BF16)
    logits = (jnp.dot(u_hi, w_hi, preferred_element_type=F32) + jnp.dot(u_lo, w_hi, preferred_element_type=F32)
              + jnp.dot(u_hi, w_lo, preferred_element_type=F32))
    e_total = br_ref.shape[0]
    lt = logits.T[:e_total, :]
    s = _sigmoid_exp(lt)
    sel = s + br_ref[...]
    e1, e2, w1, w2 = _route_rows(s, sel)
    rows = lax.broadcasted_iota(jnp.int32, r_ref.shape, 0)
    r_ref[...] = jnp.where(rows == 0, e1, jnp.where(rows == 1, e2, jnp.where(rows == 2, w1,
                                                                             jnp.where(rows == 3, w2, 0.0))))


def _norm_router(dm, layer, x, g, mod4, wr_pad, br_col, tm=256):
    row = pl.BlockSpec((tm, dm.D), lambda i: (i, 0))
    return pl.pallas_call(
        _norm_router_kernel,
        grid=(dm.T // tm,),
        in_specs=[
            row,
            pl.BlockSpec((1, dm.D), lambda i: (0, 0)),
            _mod_spec(dm, layer, 4, tm),
            _mod_spec(dm, layer, 3, tm),
            pl.BlockSpec(wr_pad.shape, lambda i: (0, 0)),
            pl.BlockSpec(br_col.shape, lambda i: (0, 0)),
        ],
        out_specs=[pl.BlockSpec((tm, dm.D // 2), lambda i: (i, 0)), pl.BlockSpec((8, tm), lambda i: (0, i))],
        out_shape=[jax.ShapeDtypeStruct((dm.T, dm.D // 2), jnp.uint32), jax.ShapeDtypeStruct((8, dm.T), F32)],
        compiler_params=_cparams(("arbitrary",)),
        name="norm_router",
    )(x, g, mod4, mod4, wr_pad, br_col)


GATHER_UNROLL = 8


def _row_gather_start(src_hbm, idx_ref, idx_base, dst_ref, sem, n_rows, priority):
    def group(g, carry):
        for k in range(GATHER_UNROLL):
            r = g * GATHER_UNROLL + k
            row = idx_ref[idx_base + r]
            pltpu.make_async_copy(src_hbm.at[pl.ds(row, 1), :], dst_ref.at[pl.ds(r, 1), :], sem).start(
                priority=priority)
        return carry

    lax.fori_loop(0, n_rows // GATHER_UNROLL, group, 0)


def _row_gather_wait(src_hbm, dst_ref, sem, n_rows):
    def group(g, carry):
        for k in range(GATHER_UNROLL):
            r = g * GATHER_UNROLL + k
            pltpu.make_async_copy(src_hbm.at[pl.ds(0, 1), :], dst_ref.at[pl.ds(r, 1), :], sem).wait()
        return carry

    lax.fori_loop(0, n_rows // GATHER_UNROLL, group, 0)


def _gathered_moe_rows(s1_ref, s2_ref, ys_hbm, buf, sem, tm, off):
    i = pl.program_id(0)
    n = pl.num_programs(0)
    slot = lax.rem(i, 2)

    def start(step, into):
        base = (step + off) * tm
        _row_gather_start(ys_hbm, s1_ref, base, buf.at[into, 0], sem.at[into], tm, 0)
        _row_gather_start(ys_hbm, s2_ref, base, buf.at[into, 1], sem.at[into], tm, 1)

    @pl.when(i == 0)
    def _():
        start(0, 0)

    @pl.when(i + 1 < n)
    def _():
        start(i + 1, 1 - slot)

    _row_gather_wait(ys_hbm, buf.at[slot, 0], sem.at[slot], tm)
    _row_gather_wait(ys_hbm, buf.at[slot, 1], sem.at[slot], tm)
    return buf[slot, 0], buf[slot, 1]


def _unpack_expert_rows(w, n_split):
    wb = w.shape[1] // n_split
    parts = []
    for j in range(n_split):
        lo, hi = _unpack_halves(w[:, j * wb:(j + 1) * wb])
        parts += [lo, hi]
    return jnp.concatenate(parts, axis=1)


def _moe_residual(s1_ref, s2_ref, ys_hbm, buf, sem, tm, off, n_split, x_ref, w1_ref, w2_ref, g2_ref):
    p1, p2 = _gathered_moe_rows(s1_ref, s2_ref, ys_hbm, buf, sem, tm, off)
    moe = w1_ref[...] * _unpack_expert_rows(p1, n_split) + w2_ref[...] * _unpack_expert_rows(p2, n_split)
    return x_ref[...] + g2_ref[...] * moe


def _combine_norm_kernel(tm, n_split, s1_ref, s2_ref, x_ref, ys_hbm, w1_ref, w2_ref, g2_ref, g_ref, sc_ref, sh_ref,
                         xo_ref, u_ref, buf, sem):
    x = _moe_residual(s1_ref, s2_ref, ys_hbm, buf, sem, tm, 0, n_split, x_ref, w1_ref, w2_ref, g2_ref)
    xo_ref[...] = x
    u_ref[...] = _norm_mod(x, g_ref[...], sc_ref[...], sh_ref[...]).astype(u_ref.dtype)


def _gather_cparams():
    return pltpu.CompilerParams(dimension_semantics=("arbitrary",), vmem_limit_bytes=VMEM_LIMIT,
                                disable_bounds_checks=True)


def _combine_scratch(dm, tm):
    return [pltpu.VMEM((2, TOP_K, tm, dm.D // 2), jnp.uint32), pltpu.SemaphoreType.DMA((2,))]


def _combine_norm(dm, layer, x, ysp, slot1, slot2, w1, w2, g_next, mod4, n_split, tm=256):
    row = pl.BlockSpec((tm, dm.D), lambda i, s1, s2: (i, 0))
    col = pl.BlockSpec((tm, 1), lambda i, s1, s2: (i, 0))
    vec = pl.BlockSpec((1, dm.D), lambda i, s1, s2: (0, 0))

    def mod(lyr, section):
        return pl.BlockSpec((None, None, 1, dm.D), lambda i, s1, s2: (lyr, _cond_idx(dm, i, tm), 0, section))

    return pl.pallas_call(
        functools.partial(_combine_norm_kernel, tm, n_split),
        grid_spec=pltpu.PrefetchScalarGridSpec(
            num_scalar_prefetch=2, grid=(dm.T // tm,),
            in_specs=[row, pl.BlockSpec(memory_space=pl.ANY), col, col, mod(layer, 5), vec,
                      mod(layer + 1, 1), mod(layer + 1, 0)],
            out_specs=[row, row],
            scratch_shapes=_combine_scratch(dm, tm)),
        out_shape=[jax.ShapeDtypeStruct((dm.T, dm.D), F32), jax.ShapeDtypeStruct((dm.T, dm.D), BF16)],
        compiler_params=_gather_cparams(),
        name="combine_norm",
    )(slot1, slot2, x, ysp, w1, w2, mod4, g_next, mod4, mod4)


def _combine_final_kernel(tm, off, n_split, s1_ref, s2_ref, x_ref, ys_hbm, w1_ref, w2_ref, g2_ref, g_ref, o_ref,
                          buf, sem):
    x = _moe_residual(s1_ref, s2_ref, ys_hbm, buf, sem, tm, off, n_split, x_ref, w1_ref, w2_ref, g2_ref)
    ms = jnp.mean(x * x, axis=-1, keepdims=True)
    o_ref[...] = x * lax.rsqrt(ms + EPS) * g_ref[...]


def _combine_final(dm, layer, x, ysp, slot1, slot2, w1, w2, g_final, mod4, n_split, row_start, n_rows, tm=256):
    off = row_start // tm
    row = pl.BlockSpec((tm, dm.D), lambda i, s1, s2: (i + off, 0))
    col = pl.BlockSpec((tm, 1), lambda i, s1, s2: (i + off, 0))
    return pl.pallas_call(
        functools.partial(_combine_final_kernel, tm, off, n_split),
        grid_spec=pltpu.PrefetchScalarGridSpec(
            num_scalar_prefetch=2, grid=(n_rows // tm,),
            in_specs=[row, pl.BlockSpec(memory_space=pl.ANY), col, col,
                      pl.BlockSpec((None, None, 1, dm.D),
                                   lambda i, s1, s2: (layer, _cond_idx(dm, i + off, tm), 0, 5)),
                      pl.BlockSpec((1, dm.D), lambda i, s1, s2: (0, 0))],
            out_specs=pl.BlockSpec((tm, dm.D), lambda i, s1, s2: (i, 0)),
            scratch_shapes=_combine_scratch(dm, tm)),
        out_shape=jax.ShapeDtypeStruct((n_rows, dm.D), F32),
        compiler_params=_gather_cparams(),
        name="combine_final",
    )(slot1, slot2, x, ysp, w1, w2, mod4, g_final)


def _mm_kernel(a_ref, w_ref, o_ref):
    o_ref[...] = jnp.dot(a_ref[...], w_ref[...], preferred_element_type=F32).astype(o_ref.dtype)


def _matmul(a, w, out_dtype, tm, tn, name):
    m, k = a.shape
    n = w.shape[1]
    return pl.pallas_call(
        _mm_kernel,
        grid=(n // tn, m // tm),
        in_specs=[pl.BlockSpec((tm, k), lambda j, i: (i, 0)), pl.BlockSpec((k, tn), lambda j, i: (0, j))],
        out_specs=pl.BlockSpec((tm, tn), lambda j, i: (i, j)),
        out_shape=jax.ShapeDtypeStruct((m, n), out_dtype),
        compiler_params=_cparams(("arbitrary", "arbitrary")),
        name=name,
    )(a, w)


def _merge_kernel(ya_ref, yb_ref, yc_ref, g0_ref, g1_ref, g2_ref, wa_ref, wb_ref, wc_ref, o_ref):
    acc = _sigmoid(g0_ref[...].astype(F32)) * jnp.dot(ya_ref[...], wa_ref[...], preferred_element_type=F32)
    acc += _sigmoid(g1_ref[...].astype(F32)) * jnp.dot(yb_ref[...], wb_ref[...], preferred_element_type=F32)
    acc += _sigmoid(g2_ref[...].astype(F32)) * jnp.dot(yc_ref[...], wc_ref[...], preferred_element_type=F32)
    o_ref[...] = acc.astype(o_ref.dtype)


def _merge(dm, ya, yb, yc, proj, wa, wb, wc, tm=512, tn=1024):
    tn = min(tn, dm.D)
    nb = dm.D // tn
    gb = dm.o_gate // tn

    def gate_spec(k):
        return pl.BlockSpec((tm, tn), lambda j, i: (i, gb + k * nb + j))

    def a_spec(kd):
        return pl.BlockSpec((tm, kd), lambda j, i: (i, 0))

    def w_spec(kd):
        return pl.BlockSpec((kd, tn), lambda j, i: (0, j))

    return pl.pallas_call(
        _merge_kernel,
        grid=(nb, dm.T // tm),
        in_specs=[a_spec(dm.d_lru), a_spec(dm.d_ssd), a_spec(dm.d_pool), gate_spec(0), gate_spec(1), gate_spec(2),
                  w_spec(dm.d_lru), w_spec(dm.d_ssd), w_spec(dm.d_pool)],
        out_specs=pl.BlockSpec((tm, tn), lambda j, i: (i, j)),
        out_shape=jax.ShapeDtypeStruct((dm.T, dm.D), BF16),
        compiler_params=_cparams(("arbitrary", "arbitrary")),
        name="merge",
    )(ya, yb, yc, proj, proj, proj, wa, wb, wc)


def _outproj_kernel(a_ref, w_ref, x_ref, g_ref, o_ref):
    o_ref[...] = x_ref[...] + g_ref[...] * jnp.dot(a_ref[...], w_ref[...], preferred_element_type=F32)


def _outproj(dm, layer, merged, w_out, x, mod4, tm=512, tn=1024):
    tn = min(tn, dm.D)
    nb = dm.D // tn
    return pl.pallas_call(
        _outproj_kernel,
        grid=(nb, dm.T // tm),
        in_specs=[
            pl.BlockSpec((tm, dm.D), lambda j, i: (i, 0)),
            pl.BlockSpec((dm.D, tn), lambda j, i: (0, j)),
            pl.BlockSpec((tm, tn), lambda j, i: (i, j)),
            pl.BlockSpec((None, None, 1, tn), lambda j, i: (layer, _cond_idx(dm, i, tm), 0, 2 * nb + j)),
        ],
        out_specs=pl.BlockSpec((tm, tn), lambda j, i: (i, j)),
        out_shape=jax.ShapeDtypeStruct((dm.T, dm.D), F32),
        compiler_params=_cparams(("arbitrary", "arbitrary")),
        name="outproj",
    )(merged, w_out, x, mod4)


def _halo_specs(dm, tl, width, col_block, tile_of):
    per = tl // HALO
    last = dm.T // HALO - 1
    cur = pl.BlockSpec((tl, width), lambda i: (tile_of(i), col_block))
    prev = pl.BlockSpec((HALO, width), lambda i: (jnp.maximum(tile_of(i) * per - 1, 0), col_block))
    nxt = pl.BlockSpec((HALO, width), lambda i: (jnp.minimum((tile_of(i) + 1) * per, last), col_block))
    return [cur, prev, nxt]


def _fill_ext(ext_ref, cur_ref, prev_ref, next_ref, is_first, is_last, tl):
    ext_ref[pl.ds(HALO, tl), :] = cur_ref[...].astype(F32)
    ext_ref[pl.ds(0, HALO), :] = jnp.where(is_first, 0.0, prev_ref[...].astype(F32))
    ext_ref[pl.ds(HALO + tl, HALO), :] = jnp.where(is_last, 0.0, next_ref[...].astype(F32))


def _conv_from_ext(ext_ref, w_ref, b_ref, tl):
    y = b_ref[...] + ext_ref[pl.ds(HALO, tl), :] * w_ref[CONV_LEFT:CONV_LEFT + 1, :]
    for k in range(CONV_W):
        if k != CONV_LEFT:
            y = y + ext_ref[pl.ds(HALO - CONV_LEFT + k, tl), :] * w_ref[k:k + 1, :]
    return y


def _lru_kernel(dm, tl, reverse, *refs):
    if reverse:
        (xa_ref, xp_ref, xn_ref, cw_ref, cb_ref, wr_ref, br_ref, wi_ref, bi_ref, lam_ref, h0_ref,
         ga_ref, hf_ref, y_ref, hT_ref, ext_ref, a_ref, bx_ref, hs_ref, h_ref) = refs
    else:
        (xa_ref, xp_ref, xn_ref, cw_ref, cb_ref, wr_ref, br_ref, wi_ref, bi_ref, lam_ref, h0_ref,
         y_ref, hT_ref, ext_ref, a_ref, bx_ref, hs_ref, h_ref) = refs
    nt = dm.T // tl
    i = pl.program_id(0)
    tile = nt - 1 - i if reverse else i
    pos, lseq, _ = _seq_tile_info(dm, tile, tl)
    is_first = pos == 0
    is_last = pos + tl == lseq
    _fill_ext(ext_ref, xa_ref, xp_ref, xn_ref, is_first, is_last, tl)
    xc = _conv_from_ext(ext_ref, cw_ref, cb_ref, tl)
    sp = _softplus(-lam_ref[...])
    bw = dm.lru_bw
    for n in range(dm.lru_blocks):
        cs = slice(n * bw, (n + 1) * bw)
        xb = xc[:, cs]
        xbh = xb.astype(BF16)
        r = _sigmoid(jnp.dot(xbh, wr_ref[n], preferred_element_type=F32) + br_ref[:, cs])
        ig = _sigmoid(jnp.dot(xbh, wi_ref[n], preferred_element_type=F32) + bi_ref[:, cs])
        log_a = (-LRU_C) * r * sp[:, cs]
        a = jnp.exp(log_a)
        a_ref[:, cs] = a
        bx_ref[:, cs] = jnp.sqrt(1.0 - a * a) * (ig * xb)

    start_of_scan = is_last if reverse else is_first

    @pl.when(start_of_scan)
    def _():
        h_ref[...] = h0_ref[...]

    def body(t, h):
        tt = tl - 1 - t if reverse else t
        h = a_ref[pl.ds(tt, 1), :] * h + bx_ref[pl.ds(tt, 1), :]
        hs_ref[pl.ds(tt, 1), :] = h
        return h

    h = lax.fori_loop(0, tl, body, h_ref[...], unroll=8)
    h_ref[...] = h
    hT_ref[...] = h
    if reverse:
        y = (hf_ref[...].astype(F32) + hs_ref[...]) * _gelu_tanh(ga_ref[...].astype(F32))
        y_ref[...] = y.astype(y_ref.dtype)
    else:
        y_ref[...] = hs_ref[...].astype(y_ref.dtype)


def _lru(dm, reverse, proj, cw, cb, wr, br, wi, bi, lam, h0, hs_fwd=None, tl=256):
    nt = dm.T // tl
    c = dm.d_lru
    tile_of = (lambda i: nt - 1 - i) if reverse else (lambda i: i)
    seq_of = lambda i: _seq_tile_info(dm, tile_of(i), tl)[2]
    full = lambda shape: pl.BlockSpec(shape, lambda i: (0,) * len(shape))
    row = pl.BlockSpec((tl, c), lambda i: (tile_of(i), 0))
    in_specs = _halo_specs(dm, tl, c, dm.o_xa // c, tile_of) + [
        full((CONV_W, c)), full((1, c)),
        full(wr.shape), full((1, c)), full(wi.shape), full((1, c)), full((1, c)),
        pl.BlockSpec((None, 1, c), lambda i: (seq_of(i), 0, 0)),
    ]
    args = [proj, proj, proj, cw, cb, wr, br, wi, bi, lam, h0]
    if reverse:
        in_specs += [pl.BlockSpec((tl, c), lambda i: (tile_of(i), dm.o_ga // c)), row]
        args += [proj, hs_fwd]
    return pl.pallas_call(
        functools.partial(_lru_kernel, dm, tl, reverse),
        grid=(nt,),
        in_specs=in_specs,
        out_specs=[row, pl.BlockSpec((None, 1, c), lambda i: (seq_of(i), 0, 0))],
        out_shape=[jax.ShapeDtypeStruct((dm.T, c), BF16), jax.ShapeDtypeStruct((dm.nseq, 1, c), F32)],
        scratch_shapes=[pltpu.VMEM((tl + 2 * HALO, c), F32), pltpu.VMEM((tl, c), F32), pltpu.VMEM((tl, c), F32),
                        pltpu.VMEM((tl, c), F32), pltpu.VMEM((1, c), F32)],
        compiler_params=_cparams(("arbitrary",)),
        name="lru_bwd" if reverse else "lru_fwd",
    )(*args)


def _pool_kernel(dm, tl, xc_ref, xp_ref, xn_ref, w_ref, s_ref, o_ref, ext_ref):
    tile = pl.program_id(0)
    pos, lseq, _ = _seq_tile_info(dm, tile, tl)
    _fill_ext(ext_ref, xc_ref, xp_ref, xn_ref, pos == 0, pos + tl == lseq, tl)
    t = pos + lax.broadcasted_iota(jnp.int32, (tl, 1), 0)
    pg = dm.pool_group
    for k, w in enumerate(POOL_WINDOWS):
        cs = slice(k * pg, (k + 1) * pg)
        tok_rows = ext_ref[pl.ds(HALO, tl), cs]
        acc = tok_rows
        for o in range(-w // 2, w // 2):
            if o != 0:
                acc = acc + ext_ref[pl.ds(HALO + o, tl), cs]
        cnt = (jnp.minimum(t + w // 2, lseq) - jnp.maximum(t - w // 2, 0)).astype(F32)
        dev = acc / cnt - tok_rows
        y = jnp.dot(dev.astype(BF16), w_ref[k], preferred_element_type=F32)
        o_ref[:, cs] = (y * s_ref[:, cs]).astype(o_ref.dtype)


def _pool(dm, proj, w_pool, scale, tl=256):
    c = dm.d_pool
    return pl.pallas_call(
        functools.partial(_pool_kernel, dm, tl),
        grid=(dm.T // tl,),
        in_specs=_halo_specs(dm, tl, c, dm.o_xc // c, lambda i: i) + [
            pl.BlockSpec(w_pool.shape, lambda i: (0, 0, 0)), pl.BlockSpec((1, c), lambda i: (0, 0))],
        out_specs=pl.BlockSpec((tl, c), lambda i: (i, 0)),
        out_shape=jax.ShapeDtypeStruct((dm.T, c), BF16),
        scratch_shapes=[pltpu.VMEM((tl + 2 * HALO, c), F32)],
        compiler_params=_cparams(("arbitrary",)),
        name="pool",
    )(proj, proj, proj, w_pool, scale)


def _dot_exact(a, b):
    return jnp.dot(a, b, preferred_element_type=F32, precision=lax.Precision.HIGHEST)


def _ssd_kernel(dm, q, reverse, *refs):
    if reverse:
        (xact_ref, dt_ref, dtb_ref, alog_ref, h0_ref, z_ref, yf_ref, ng_ref,
         y_ref, hT_ref, h_ref, yacc_ref) = refs
    else:
        (xs_ref, xsp_ref, xsn_ref, bc_ref, bcp_ref, bcn_ref, dt_ref, cwx_ref, cbx_ref, cwb_ref, cbb_ref,
         dtb_ref, alog_ref, h0_ref, dsk_ref, y_ref, hT_ref, xact_ref, extx_ref, extb_ref, h_ref) = refs
    nt = dm.T // q
    i = pl.program_id(0)
    tile = nt - 1 - i if reverse else i
    pos, lseq, _ = _seq_tile_info(dm, tile, q)
    is_first = pos == 0
    is_last = pos + q == lseq
    d_ssd, n, g_cnt, r_cnt, p = dm.d_ssd, dm.N, dm.G, dm.R, dm.P
    gn = g_cnt * n
    pair = 2 * p
    assert pair == LANES and n == LANES and q == LANES
    if reverse:
        xs = xact_ref[:, :d_ssd].astype(F32)
        bc = xact_ref[:, d_ssd:].astype(F32)
    else:
        _fill_ext(extx_ref, xs_ref, xsp_ref, xsn_ref, is_first, is_last, q)
        _fill_ext(extb_ref, bc_ref, bcp_ref, bcn_ref, is_first, is_last, q)
        xs = _silu(_conv_from_ext(extx_ref, cwx_ref, cbx_ref, q))
        bc = _silu(_conv_from_ext(extb_ref, cwb_ref, cbb_ref, q))
        xact_ref[:, :d_ssd] = xs.astype(xact_ref.dtype)
        xact_ref[:, d_ssd:] = bc.astype(xact_ref.dtype)

    @pl.when(is_last if reverse else is_first)
    def _():
        h_ref[...] = h0_ref[...]

    dt = _softplus(dt_ref[...] + dtb_ref[...])
    a_neg = -jnp.exp(alog_ref[...])
    dta = dt * a_neg
    ri = lax.broadcasted_iota(jnp.int32, (q, q), 0)
    ci = lax.broadcasted_iota(jnp.int32, (q, q), 1)
    valid = (ci >= ri) if reverse else (ci <= ri)
    a_cs = _dot_exact(valid.astype(F32), dta)
    a_end = a_cs[0:1, :] if reverse else a_cs[q - 1:q, :]
    w_state = dt * jnp.exp(a_end - a_cs)
    src_t = (a_cs - jnp.log(dt)).T
    cd_t = jnp.broadcast_to(jnp.exp(a_end), (q, LANES)).T
    lane = lax.broadcasted_iota(jnp.int32, (q, LANES), 1)
    low_half = lane < p

    for g in range(g_cnt):
        b_h = bc[:, g * n:(g + 1) * n].astype(BF16)
        c_h = bc[:, gn + g * n:gn + (g + 1) * n].astype(BF16)
        scores = lax.dot_general(c_h, b_h, (((1,), (1,)), ((), ())), preferred_element_type=F32)
        r0 = g * r_cnt * p
        h_grp = h_ref[pl.ds(r0, r_cnt * p), :].astype(BF16)
        y_off = lax.dot_general(c_h, h_grp, (((1,), (1,)), ((), ())), preferred_element_type=F32)
        for k in range(r_cnt // 2):
            c0 = r0 + 2 * k * p
            hd0 = g * r_cnt + 2 * k
            xs_pair = xs[:, c0:c0 + pair]
            xs_h = xs_pair.astype(BF16)
            ys, eas, ws = [], [], []
            for s in range(2):
                hd = hd0 + s
                col = jnp.broadcast_to(a_cs[:, hd:hd + 1], (q, q))
                row = jnp.broadcast_to(src_t[hd:hd + 1, :], (q, q))
                m = (scores * jnp.exp(jnp.where(valid, col - row, -jnp.inf))).astype(BF16)
                ys.append(jnp.dot(m, xs_h, preferred_element_type=F32))
                eas.append(jnp.exp(col))
                ws.append(jnp.broadcast_to(w_state[:, hd:hd + 1], (q, LANES)))
            y_pair = (jnp.where(low_half, ys[0], ys[1])
                      + y_off[:, c0 - r0:c0 - r0 + pair] * jnp.where(low_half, eas[0], eas[1]))
            xsw = (xs_pair * jnp.where(low_half, ws[0], ws[1])).astype(BF16)
            st = lax.dot_general(xsw, b_h, (((0,), (0,)), ((), ())), preferred_element_type=F32)
            dec = jnp.concatenate([jnp.broadcast_to(cd_t[hd0:hd0 + 1, :], (p, n)),
                                   jnp.broadcast_to(cd_t[hd0 + 1:hd0 + 2, :], (p, n))], axis=0)
            h_ref[pl.ds(c0, pair), :] = dec * h_ref[pl.ds(c0, pair), :] + st
            if reverse:
                yacc_ref[:, c0:c0 + pair] = y_pair
            else:
                y_ref[:, c0:c0 + pair] = (y_pair + dsk_ref[:, c0:c0 + pair] * xs_pair).astype(y_ref.dtype)

    hT_ref[...] = h_ref[...]
    if reverse:
        gw = d_ssd // g_cnt
        for g in range(g_cnt):
            cs = slice(g * gw, (g + 1) * gw)
            y = (yacc_ref[:, cs] + yf_ref[:, cs]) * _silu(z_ref[:, cs].astype(F32))
            ms = jnp.mean(y * y, axis=-1, keepdims=True)
            y_ref[:, cs] = (y * lax.rsqrt(ms + EPS) * ng_ref[:, cs]).astype(y_ref.dtype)


def _ssd(dm, reverse, proj, dtp, cw, cb, dtb, alog, h0, extra, fwd=None, q=128):
    nt = dm.T // q
    tile_of = (lambda i: nt - 1 - i) if reverse else (lambda i: i)
    seq_of = lambda i: _seq_tile_info(dm, tile_of(i), q)[2]
    full = lambda shape: pl.BlockSpec(shape, lambda i: (0,) * len(shape))
    hp = dm.H * dm.P
    rowy = pl.BlockSpec((q, dm.d_ssd), lambda i: (tile_of(i), 0))
    rowact = pl.BlockSpec((q, dm.d_xbc), lambda i: (tile_of(i), 0))
    state = pl.BlockSpec((None, hp, dm.N), lambda i: (seq_of(i), 0, 0))
    dt_spec = pl.BlockSpec((q, LANES), lambda i: (tile_of(i), 1 if reverse else 0))
    vec = [full((1, LANES)), full((1, LANES)), state]
    if reverse:
        y_fwd, xact = fwd
        in_specs = [rowact, dt_spec] + vec + [
            pl.BlockSpec((q, dm.d_ssd), lambda i: (tile_of(i), dm.o_z // dm.d_ssd)), rowy, full((1, dm.d_ssd))]
        args = [xact, dtp, dtb, alog, h0, proj, y_fwd, extra]
        out_specs = [rowy, state]
        out_shape = [jax.ShapeDtypeStruct((dm.T, dm.d_ssd), BF16), jax.ShapeDtypeStruct((dm.nseq, hp, dm.N), F32)]
        scratch = [pltpu.VMEM((hp, dm.N), F32), pltpu.VMEM((q, dm.d_ssd), F32)]
    else:
        in_specs = (_halo_specs(dm, q, dm.d_ssd, dm.o_xs // dm.d_ssd, tile_of)
                    + _halo_specs(dm, q, dm.d_bc, dm.o_bc // dm.d_bc, tile_of)
                    + [dt_spec, full((CONV_W, dm.d_ssd)), full((1, dm.d_ssd)), full((CONV_W, dm.d_bc)),
                       full((1, dm.d_bc))] + vec + [full((1, dm.d_ssd))])
        args = [proj] * 6 + [dtp, cw[:, :dm.d_ssd], cb[:, :dm.d_ssd], cw[:, dm.d_ssd:], cb[:, dm.d_ssd:],
                             dtb, alog, h0, extra]
        out_specs = [rowy, state, rowact]
        out_shape = [jax.ShapeDtypeStruct((dm.T, dm.d_ssd), F32), jax.ShapeDtypeStruct((dm.nseq, hp, dm.N), F32),
                     jax.ShapeDtypeStruct((dm.T, dm.d_xbc), BF16)]
        scratch = [pltpu.VMEM((q + 2 * HALO, dm.d_ssd), F32), pltpu.VMEM((q + 2 * HALO, dm.d_bc), F32),
                   pltpu.VMEM((hp, dm.N), F32)]
    return pl.pallas_call(
        functools.partial(_ssd_kernel, dm, q, reverse),
        grid=(nt,),
        in_specs=in_specs,
        out_specs=out_specs,
        out_shape=out_shape,
        scratch_shapes=scratch,
        compiler_params=_cparams(("arbitrary",)),
        name="ssd_bwd" if reverse else "ssd_fwd",
    )(*args)


def _moe_dispatch_kernel(tm, n_exp, s1_ref, s2_ref, lo_ref, hi_ref, u_ref, o_hbm, zrow, sem, zsem):
    i = pl.program_id(0)
    base = i * tm

    def start_group(g, carry):
        for k in range(GATHER_UNROLL):
            r = g * GATHER_UNROLL + k
            src = u_ref.at[pl.ds(r, 1), :]
            pltpu.make_async_copy(src, o_hbm.at[pl.ds(s1_ref[base + r], 1), :], sem).start(priority=0)
            pltpu.make_async_copy(src, o_hbm.at[pl.ds(s2_ref[base + r], 1), :], sem).start(priority=1)
        return carry

    def wait_group(g, carry):
        for k in range(GATHER_UNROLL):
            r = g * GATHER_UNROLL + k
            src = u_ref.at[pl.ds(r, 1), :]
            pltpu.make_async_copy(src, o_hbm.at[pl.ds(0, 1), :], sem).wait()
            pltpu.make_async_copy(src, o_hbm.at[pl.ds(0, 1), :], sem).wait()
        return carry

    lax.fori_loop(0, tm // GATHER_UNROLL, start_group, 0)
    lax.fori_loop(0, tm // GATHER_UNROLL, wait_group, 0)

    @pl.when(i == pl.num_programs(0) - 1)
    def _():
        zrow[...] = jnp.zeros_like(zrow)
        for e in range(n_exp):
            lo = lo_ref[e]
            hi = hi_ref[e]

            def zstart(r, carry):
                pltpu.make_async_copy(zrow, o_hbm.at[pl.ds(r, 1), :], zsem).start()
                return carry

            def zwait(r, carry):
                pltpu.make_async_copy(zrow, o_hbm.at[pl.ds(0, 1), :], zsem).wait()
                return carry

            lax.fori_loop(lo, hi, zstart, 0)
            lax.fori_loop(lo, hi, zwait, 0)


def _moe_dispatch(u2p, slot1, slot2, pad_lo, pad_hi, p_pad, tm):
    t, words = u2p.shape
    n_exp = pad_lo.shape[0]
    return pl.pallas_call(
        functools.partial(_moe_dispatch_kernel, tm, n_exp),
        grid_spec=pltpu.PrefetchScalarGridSpec(
            num_scalar_prefetch=4, grid=(t // tm,),
            in_specs=[pl.BlockSpec((tm, words), lambda i, s1, s2, lo, hi: (i, 0))],
            out_specs=pl.BlockSpec(memory_space=pl.ANY),
            scratch_shapes=[pltpu.VMEM((1, words), jnp.uint32), pltpu.SemaphoreType.DMA(()),
                            pltpu.SemaphoreType.DMA(())]),
        out_shape=jax.ShapeDtypeStruct((p_pad, words), jnp.uint32),
        compiler_params=_gather_cparams(),
        name="moe_dispatch",
    )(slot1, slot2, pad_lo, pad_hi, u2p)


def _expert_changed(te_ref, i):
    return jnp.logical_or(i == 0, te_ref[i] != te_ref[jnp.maximum(i - 1, 0)])


def _expert_up_kernel(te_ref, nu_ref, x_ref, wg_ref, wu_ref, h_ref, wg_s, wu_s):
    i = pl.program_id(1)

    @pl.when(jnp.logical_and(i < nu_ref[0], _expert_changed(te_ref, i)))
    def _():
        wg_s[...] = wg_ref[...].astype(BF16)
        wu_s[...] = wu_ref[...].astype(BF16)

    @pl.when(i < nu_ref[0])
    def _():
        lo, hi = _unpack_halves(x_ref[...])
        lo = lo.astype(BF16)
        hi = hi.astype(BF16)
        half = lo.shape[1]
        a = (jnp.dot(lo, wg_s[:half, :], preferred_element_type=F32)
             + jnp.dot(hi, wg_s[half:, :], preferred_element_type=F32))
        b = (jnp.dot(lo, wu_s[:half, :], preferred_element_type=F32)
             + jnp.dot(hi, wu_s[half:, :], preferred_element_type=F32))
        h_ref[...] = (_silu(a) * b).astype(h_ref.dtype)

    @pl.when(i >= nu_ref[0])
    def _():
        h_ref[...] = jnp.zeros_like(h_ref)


def _expert_down_kernel(te_ref, nu_ref, h_ref, wd_ref, y_ref, wd_s):
    i = pl.program_id(1)

    @pl.when(jnp.logical_and(i < nu_ref[0], _expert_changed(te_ref, i)))
    def _():
        wd_s[...] = wd_ref[...].astype(BF16)

    @pl.when(i < nu_ref[0])
    def _():
        y_ref[...] = _pack_halves(jnp.dot(h_ref[...], wd_s[...], preferred_element_type=F32))

    @pl.when(i >= nu_ref[0])
    def _():
        y_ref[...] = jnp.zeros_like(y_ref)


def _experts(dm, layer, xsp, w_gate, w_up, w_down, tile_expert, n_used, tm, n_split):
    p_pad = xsp.shape[0]
    n_tiles = p_pad // tm
    d, f = dm.D, dm.F
    fh, dh = f // n_split, d // n_split
    h = pl.pallas_call(
        _expert_up_kernel,
        grid_spec=pltpu.PrefetchScalarGridSpec(
            num_scalar_prefetch=2, grid=(n_split, n_tiles),
            in_specs=[pl.BlockSpec((tm, d // 2), lambda j, i, te, nu: (i, 0)),
                      pl.BlockSpec((None, None, d, fh), lambda j, i, te, nu: (layer, te[i], 0, j)),
                      pl.BlockSpec((None, None, d, fh), lambda j, i, te, nu: (layer, te[i], 0, j))],
            out_specs=pl.BlockSpec((tm, fh), lambda j, i, te, nu: (i, j)),
            scratch_shapes=[pltpu.VMEM((d, fh), BF16), pltpu.VMEM((d, fh), BF16)]),
        out_shape=jax.ShapeDtypeStruct((p_pad, f), BF16),
        compiler_params=_cparams(("arbitrary", "arbitrary")),
        name="expert_up",
    )(tile_expert, n_used, xsp, w_gate, w_up)
    return pl.pallas_call(
        _expert_down_kernel,
        grid_spec=pltpu.PrefetchScalarGridSpec(
            num_scalar_prefetch=2, grid=(n_split, n_tiles),
            in_specs=[pl.BlockSpec((tm, f), lambda j, i, te, nu: (i, 0)),
                      pl.BlockSpec((None, None, f, dh), lambda j, i, te, nu: (layer, te[i], 0, j))],
            out_specs=pl.BlockSpec((tm, dh // 2), lambda j, i, te, nu: (i, j)),
            scratch_shapes=[pltpu.VMEM((f, dh), BF16)]),
        out_shape=jax.ShapeDtypeStruct((p_pad, d // 2), jnp.uint32),
        compiler_params=_cparams(("arbitrary", "arbitrary")),
        name="expert_down",
    )(tile_expert, n_used, h, w_down)


def _dispatch_plan(dm, route, tm):
    t = dm.T
    e_cnt = dm.E
    e_idx = route[0:2].astype(jnp.int32).reshape(-1)
    onehot = (e_idx[:, None] == jnp.arange(e_cnt, dtype=jnp.int32)[None, :]).astype(jnp.int32)
    rank = jnp.sum((jnp.cumsum(onehot, axis=0) - onehot) * onehot, axis=1)
    counts = jnp.sum(onehot, axis=0)
    tiles_per = (counts + tm - 1) // tm
    tile_end = jnp.cumsum(tiles_per)
    start = (tile_end - tiles_per) * tm
    slot = jnp.sum(onehot * start[None, :], axis=1) + rank
    n_tiles = (TOP_K * t) // tm + e_cnt
    p_pad = n_tiles * tm
    tile_ids = jnp.arange(n_tiles, dtype=jnp.int32)
    tile_expert = jnp.minimum(jnp.sum((tile_end[None, :] <= tile_ids[:, None]).astype(jnp.int32), axis=1),
                              e_cnt - 1)
    n_used = tile_end[-1:].astype(jnp.int32)
    pad_lo = jnp.concatenate([start + counts, tile_end[-1:] * tm]).astype(jnp.int32)
    pad_hi = jnp.concatenate([tile_end * tm, jnp.full((1,), p_pad, tile_end.dtype)]).astype(jnp.int32)
    return p_pad, pad_lo, pad_hi, tile_expert, n_used, slot[:t], slot[t:]


def _grid_pos_embed(n_tokens, d_model):
    rows = n_tokens // GRID_W
    row = jnp.repeat(jnp.arange(rows), GRID_W).astype(F32)
    col = jnp.tile(jnp.arange(GRID_W), rows).astype(F32)
    quarter = d_model // 4
    omega = 1.0 / (10000.0 ** (jnp.arange(quarter, dtype=F32) / quarter))
    ang_r = row[:, None] * omega[None, :]
    ang_c = col[:, None] * omega[None, :]
    return jnp.concatenate([jnp.sin(ang_r), jnp.cos(ang_r), jnp.sin(ang_c), jnp.cos(ang_c)], axis=-1)


def _pad_lanes(v, fill=0.0):
    return jnp.pad(v.astype(F32), (0, LANES - v.shape[0]), constant_values=fill).reshape(1, LANES)


def kernel(x_prompt, x_sample, state_lru, state_ssd, c, c_ctx, w_mod, b_mod, norm1_g, w_in, conv_a_w, conv_a_b, w_r, b_r, w_i, b_i, lru_lambda, w_oa, conv_b_w, conv_b_b, dt_bias, a_log, d_skip, ssd_norm_g, w_ob, w_pool, pool_scale, w_oc, w_out, norm2_g, w_router, b_router, w_gate, w_up, w_down, final_g):
    dm = Dims(x_prompt, x_sample, state_ssd, w_r, w_pool, w_gate, dt_bias, conv_b_w, w_oa, w_ob, w_oc)
    d = dm.D
    moe_tm = min(512, dm.Lp)

    ncond = 16
    cond = jnp.zeros((ncond, d), F32).at[:dm.Bs].set(c).at[dm.Bs].set(c_ctx)
    mod4 = _mod_table(cond, w_mod, b_mod).reshape(dm.depth, ncond, 1, 6 * d)

    pos = _grid_pos_embed(dm.Ls, d)
    x, u = _embed_norm(dm, x_sample.reshape(dm.Ts, d), x_prompt.reshape(dm.Tp, d), pos,
                       norm1_g[0].reshape(1, d), mod4)

    wr_pad = jnp.pad(w_router, ((0, 0), (0, LANES - dm.E)))
    br_col = b_router.reshape(dm.E, 1).astype(F32)
    hp = dm.H * dm.P
    lru_states, ssd_states = [], []
    out = None
    for l in range(dm.depth):
        wl = w_in[l]
        w_main = jnp.concatenate([wl[:, :dm.o_dt], wl[:, dm.o_dt + 2 * dm.H:]], axis=1).astype(BF16)
        lane_pad = ((0, 0), (0, LANES - dm.H))
        w_dt = jnp.concatenate([jnp.pad(wl[:, dm.o_dt:dm.o_dt + dm.H], lane_pad),
                                jnp.pad(wl[:, dm.o_dt + dm.H:dm.o_dt + 2 * dm.H], lane_pad)], axis=1).astype(BF16)
        proj = _matmul(u, w_main, BF16, min(1024, dm.Lp), min(1024, dm.d_lru), "in_proj")
        dtp = _matmul(u, w_dt, F32, min(512, dm.Lp), 2 * LANES, "dt_proj")

        h0_lru = [jnp.concatenate([state_lru[:, l, dd], jnp.zeros((dm.Bp, dm.d_lru), F32)], axis=0)
                  .reshape(dm.nseq, 1, dm.d_lru) for dd in range(2)]
        lru_args = lambda dd: (conv_a_w[l], conv_a_b[l].reshape(1, -1), w_r[l, dd].astype(BF16),
                               b_r[l, dd].reshape(1, -1), w_i[l, dd].astype(BF16), b_i[l, dd].reshape(1, -1),
                               lru_lambda[l, dd].reshape(1, -1), h0_lru[dd])
        hs_f, lru_tf = _lru(dm, False, proj, *lru_args(0))
        ya, lru_tb = _lru(dm, True, proj, *lru_args(1), hs_fwd=hs_f)
        lru_states.append(jnp.stack([lru_tf[dm.Bs:, 0], lru_tb[dm.Bs:, 0]], axis=1))

        h0_ssd = [jnp.concatenate([state_ssd[:, l, dd].reshape(dm.Bs, hp, dm.N),
                                   jnp.zeros((dm.Bp, hp, dm.N), F32)], axis=0) for dd in range(2)]
        ssd_args = lambda dd: (conv_b_w[l], conv_b_b[l].reshape(1, -1), _pad_lanes(dt_bias[l, dd]),
                               _pad_lanes(a_log[l, dd]), h0_ssd[dd])
        dsk = jnp.repeat(d_skip[l].astype(F32), dm.P).reshape(1, dm.d_ssd)
        y_f, ssd_tf, xact = _ssd(dm, False, proj, dtp, *ssd_args(0), dsk)
        yb, ssd_tb = _ssd(dm, True, proj, dtp, *ssd_args(1), ssd_norm_g[l].reshape(1, -1), fwd=(y_f, xact))
        ssd_states.append(jnp.stack([ssd_tf[dm.Bs:], ssd_tb[dm.Bs:]], axis=1)
                          .reshape(dm.Bp, 2, dm.H, dm.P, dm.N))

        yc = _pool(dm, proj, w_pool[l].astype(BF16), pool_scale[l].reshape(1, -1))

        merged = _merge(dm, ya, yb, yc, proj, w_oa[l].astype(BF16), w_ob[l].astype(BF16),
                        w_oc[l].astype(BF16), tm=min(512, dm.Lp))
        x = _outproj(dm, l, merged, w_out[l].astype(BF16), x, mod4, tm=min(512, dm.Lp))

        u2p, route = _norm_router(dm, l, x, norm2_g[l].reshape(1, d), mod4, wr_pad, br_col)
        p_pad, pad_lo, pad_hi, tile_expert, n_used, slot1, slot2 = _dispatch_plan(dm, route, moe_tm)
        n_split = 2
        xsp = _moe_dispatch(u2p, slot1, slot2, pad_lo, pad_hi, p_pad, moe_tm)
        ysp = _experts(dm, l, xsp, w_gate, w_up, w_down, tile_expert, n_used, moe_tm, n_split)
        w1 = route[2].reshape(dm.T, 1)
        w2 = route[3].reshape(dm.T, 1)
        if l + 1 < dm.depth:
            x, u = _combine_norm(dm, l, x, ysp, slot1, slot2, w1, w2, norm1_g[l + 1].reshape(1, d), mod4, n_split)
        else:
            fg = final_g.reshape(1, d)
            out = (_combine_final(dm, l, x, ysp, slot1, slot2, w1, w2, fg, mod4, n_split, 0, dm.Ts),
                   _combine_final(dm, l, x, ysp, slot1, slot2, w1, w2, fg, mod4, n_split, dm.Ts, dm.Tp))

    y_sample = out[0].reshape(dm.Bs, dm.Ls, d)
    y_prompt = out[1].reshape(dm.Bp, dm.Lp, d)
    new_state_lru = jnp.stack(lru_states, axis=1).astype(x_prompt.dtype)
    new_state_ssd = jnp.stack(ssd_states, axis=1).astype(x_prompt.dtype)
    return (y_prompt, y_sample, new_state_lru, new_state_ssd)
```

```python
import functools
import math

import numpy as np
import jax
import jax.numpy as jnp
from jax import lax
from jax.experimental import pallas as pl
from jax.experimental.pallas import tpu as pltpu

F32 = jnp.float32
BF16 = jnp.bfloat16

EPS = 1e-6
GRID_W = 64
CONV_W = 4
CONV_LEFT = 1
LRU_C = 8.0
POOL_WINDOWS = (2, 4, 8, 16)
N_EXPERT_GROUPS = 4
TOP_K = 2

LANES = 128
HALO = 16
VMEM_LIMIT = 56 * 1024 * 1024


def _cparams(sem):
    return pltpu.CompilerParams(dimension_semantics=sem, vmem_limit_bytes=VMEM_LIMIT)


def _sigmoid_exp(x):
    return 1.0 / (1.0 + jnp.exp(-x))


def _sigmoid(x):
    return 0.5 * jnp.tanh(0.5 * x) + 0.5


def _silu(x):
    return x * _sigmoid(x)


def _softplus(x):
    return jnp.maximum(x, 0.0) + jnp.log(1.0 + jnp.exp(-jnp.abs(x)))


def _gelu_tanh(x):
    return x * (0.5 * (1.0 + jnp.tanh(math.sqrt(2.0 / math.pi) * (x + 0.044715 * (x * x * x)))))


def _pack_halves(x):
    c = x.shape[1] // 2
    lo = lax.bitcast_convert_type(x[:, :c].astype(BF16).astype(F32), jnp.uint32)
    hi = lax.bitcast_convert_type(x[:, c:].astype(BF16).astype(F32), jnp.uint32)
    return (hi & jnp.uint32(0xFFFF0000)) | (lo >> 16)


def _unpack_halves(w):
    lo = lax.bitcast_convert_type(w << 16, F32)
    hi = lax.bitcast_convert_type(w & jnp.uint32(0xFFFF0000), F32)
    return lo, hi


class Dims:
    def __init__(self, x_prompt, x_sample, state_ssd, w_r, w_pool, w_gate, dt_bias, conv_b_w, w_oa, w_ob, w_oc):
        self.Bp, self.Lp, self.D = x_prompt.shape
        self.Bs, self.Ls, _ = x_sample.shape
        self.depth = w_r.shape[0]
        self.Ts = self.Bs * self.Ls
        self.Tp = self.Bp * self.Lp
        self.T = self.Ts + self.Tp
        self.nseq = self.Bs + self.Bp
        self.d_lru = w_oa.shape[1]
        self.lru_blocks = w_r.shape[2]
        self.lru_bw = w_r.shape[3]
        self.d_ssd = w_ob.shape[1]
        self.H = dt_bias.shape[2]
        self.P = self.d_ssd // self.H
        self.N = state_ssd.shape[-1]
        self.d_xbc = conv_b_w.shape[2]
        self.G = (self.d_xbc - self.d_ssd) // (2 * self.N)
        self.R = self.H // self.G
        self.d_pool = w_oc.shape[1]
        self.pool_group = w_pool.shape[2]
        self.E = w_gate.shape[1]
        self.F = w_gate.shape[3]
        self.d_bc = 2 * self.G * self.N
        self.o_xa = 0
        self.o_ga = self.o_xa + self.d_lru
        self.o_z = self.o_ga + self.d_lru
        self.o_xs = self.o_z + self.d_ssd
        self.o_bc = self.o_xs + self.d_ssd
        self.o_dt = self.o_bc + self.d_bc
        self.o_xc = self.o_bc + self.d_bc
        self.o_gate = self.o_xc + self.d_pool
        self.n_main = self.o_gate + 3 * self.D


def _seq_tile_info(dm, tile, tl):
    row0 = tile * tl
    in_sample = row0 < dm.Ts
    rp = jnp.maximum(row0 - dm.Ts, 0)
    pos = jnp.where(in_sample, row0 % dm.Ls, rp % dm.Lp)
    lseq = jnp.where(in_sample, dm.Ls, dm.Lp)
    seq = jnp.where(in_sample, row0 // dm.Ls, dm.Bs + rp // dm.Lp)
    return pos, lseq, seq


def _cond_idx(dm, i, tm):
    return jnp.minimum((i * tm) // dm.Ls, dm.Bs)


def _mod_kernel(c_ref, w_ref, b_ref, o_ref):
    c = c_ref[...]
    a = _silu(c).astype(BF16)
    o_ref[...] = jnp.dot(a, w_ref[...].astype(BF16), preferred_element_type=F32) + b_ref[...]


def _mod_table(cond, w_mod, b_mod, tn=1024):
    depth, d, n6 = w_mod.shape
    nc = cond.shape[0]
    tn = math.gcd(n6, tn)
    return pl.pallas_call(
        _mod_kernel,
        grid=(depth, n6 // tn),
        in_specs=[
            pl.BlockSpec((nc, d), lambda l, j: (0, 0)),
            pl.BlockSpec((None, d, tn), lambda l, j: (l, 0, j)),
            pl.BlockSpec((None, 1, tn), lambda l, j: (l, 0, j)),
        ],
        out_specs=pl.BlockSpec((None, nc, tn), lambda l, j: (l, 0, j)),
        out_shape=jax.ShapeDtypeStruct((depth, nc, n6), F32),
        compiler_params=_cparams(("arbitrary", "arbitrary")),
        name="mod_table",
    )(cond, w_mod, b_mod.reshape(depth, 1, n6))


def _norm_mod(x, g, sc, sh):
    ms = jnp.mean(x * x, axis=-1, keepdims=True)
    y = x * lax.rsqrt(ms + EPS) * g
    return y * (1.0 + sc) + sh


def _mod_spec(dm, layer, section, tm):
    return pl.BlockSpec((None, None, 1, dm.D), lambda i: (layer, _cond_idx(dm, i, tm), 0, section))


def _embed_norm_kernel(dm, tm, xs_ref, xp_ref, pos_ref, g_ref, sc_ref, sh_ref, x_ref, u_ref):
    i = pl.program_id(0)
    n_s = dm.Ts // tm

    @pl.when(i < n_s)
    def _():
        x_ref[...] = xs_ref[...] + pos_ref[...]

    @pl.when(i >= n_s)
    def _():
        x_ref[...] = xp_ref[...]

    u_ref[...] = _norm_mod(x_ref[...], g_ref[...], sc_ref[...], sh_ref[...]).astype(u_ref.dtype)


def _embed_norm(dm, xs2, xp2, pos, g, mod4, tm=256):
    n_s = dm.Ts // tm
    n_l = dm.Ls // tm
    row = pl.BlockSpec((tm, dm.D), lambda i: (i, 0))
    return pl.pallas_call(
        functools.partial(_embed_norm_kernel, dm, tm),
        grid=(dm.T // tm,),
        in_specs=[
            pl.BlockSpec((tm, dm.D), lambda i: (jnp.minimum(i, n_s - 1), 0)),
            pl.BlockSpec((tm, dm.D), lambda i: (jnp.maximum(i - n_s, 0), 0)),
            pl.BlockSpec((tm, dm.D), lambda i: (i % n_l, 0)),
            pl.BlockSpec((1, dm.D), lambda i: (0, 0)),
            _mod_spec(dm, 0, 1, tm),
            _mod_spec(dm, 0, 0, tm),
        ],
        out_specs=[row, row],
        out_shape=[jax.ShapeDtypeStruct((dm.T, dm.D), F32), jax.ShapeDtypeStruct((dm.T, dm.D), BF16)],
        compiler_params=_cparams(("arbitrary",)),
        name="embed_norm",
    )(xs2, xp2, pos, g, mod4, mod4)


def _route_rows(s, sel):
    e_total = s.shape[0]
    per = e_total // N_EXPERT_GROUPS
    assert per == 4 and TOP_K == 2
    srow = [s[e:e + 1, :] for e in range(e_total)]
    vrow = [sel[e:e + 1, :] for e in range(e_total)]
    scores = []
    for gi in range(N_EXPERT_GROUPS):
        a, b, c, d = vrow[4 * gi:4 * gi + 4]
        hi1, lo1 = jnp.maximum(a, b), jnp.minimum(a, b)
        hi2, lo2 = jnp.maximum(c, d), jnp.minimum(c, d)
        top = jnp.maximum(hi1, hi2)
        second = jnp.maximum(jnp.minimum(hi1, hi2), jnp.maximum(lo1, lo2))
        scores.append(top + second)
    g = jnp.zeros_like(scores[0])
    best = scores[0]
    for gi in range(1, N_EXPERT_GROUPS):
        upd = scores[gi] > best
        g = jnp.where(upd, float(gi), g)
        best = jnp.where(upd, scores[gi], best)

    def pick(rows, j):
        out = rows[j]
        for gi in range(1, N_EXPERT_GROUPS):
            out = jnp.where(g == float(gi), rows[4 * gi + j], out)
        return out

    v = [pick(vrow, j) for j in range(4)]
    sv = [pick(srow, j) for j in range(4)]
    i1 = jnp.zeros_like(g)
    b1 = v[0]
    for j in range(1, 4):
        upd = v[j] > b1
        i1 = jnp.where(upd, float(j), i1)
        b1 = jnp.where(upd, v[j], b1)
    i2 = jnp.zeros_like(g)
    b2 = jnp.full_like(g, -jnp.inf)
    for j in range(4):
        upd = jnp.logical_and(i1 != float(j), v[j] > b2)
        i2 = jnp.where(upd, float(j), i2)
        b2 = jnp.where(upd, v[j], b2)

    def pick_idx(idx):
        out = sv[0]
        for j in range(1, 4):
            out = jnp.where(idx == float(j), sv[j], out)
        return out

    s1, s2 = pick_idx(i1), pick_idx(i2)
    tot = s1 + s2
    return g * 4.0 + i1, g * 4.0 + i2, s1 / tot, s2 / tot


def _norm_router_kernel(x_ref, g_ref, sc_ref, sh_ref, wr_ref, br_ref, u_ref, r_ref):
    u = _norm_mod(x_ref[...], g_ref[...], sc_ref[...], sh_ref[...])
    u_hi = u.astype(BF16)
    u_ref[...] = _pack_halves(u)
    u_lo = (u - u_hi.astype(F32)).astype(BF16)
    w = wr_ref[...]
    w_hi = w.astype(BF16)
    w_lo = (w - w_hi.astype(F32)).astype(BF16)
    logits = (jnp.dot(u_hi, w_hi, preferred_element_type=F32) + jnp.dot(u_lo, w_hi, preferred_element_type=F32)
              + jnp.dot(u_hi, w_lo, preferred_element_type=F32))
    e_total = br_ref.shape[0]
    lt = logits.T[:e_total, :]
    s = _sigmoid_exp(lt)
    sel = s + br_ref[...]
    e1, e2, w1, w2 = _route_rows(s, sel)
    rows = lax.broadcasted_iota(jnp.int32, r_ref.shape, 0)
    r_ref[...] = jnp.where(rows == 0, e1, jnp.where(rows == 1, e2, jnp.where(rows == 2, w1,
                                                                             jnp.where(rows == 3, w2, 0.0))))


def _norm_router(dm, layer, x, g, mod4, wr_pad, br_col, tm=256):
    row = pl.BlockSpec((tm, dm.D), lambda i: (i, 0))
    return pl.pallas_call(
        _norm_router_kernel,
        grid=(dm.T // tm,),
        in_specs=[
            row,
            pl.BlockSpec((1, dm.D), lambda i: (0, 0)),
            _mod_spec(dm, layer, 4, tm),
            _mod_spec(dm, layer, 3, tm),
            pl.BlockSpec(wr_pad.shape, lambda i: (0, 0)),
            pl.BlockSpec(br_col.shape, lambda i: (0, 0)),
        ],
        out_specs=[pl.BlockSpec((tm, dm.D // 2), lambda i: (i, 0)), pl.BlockSpec((8, tm), lambda i: (0, i))],
        out_shape=[jax.ShapeDtypeStruct((dm.T, dm.D // 2), jnp.uint32), jax.ShapeDtypeStruct((8, dm.T), F32)],
        compiler_params=_cparams(("arbitrary",)),
        name="norm_router",
    )(x, g, mod4, mod4, wr_pad, br_col)


GATHER_UNROLL = 8


def _row_gather_start(src_hbm, idx_ref, idx_base, dst_ref, sem, n_rows, priority):
    def group(g, carry):
        for k in range(GATHER_UNROLL):
            r = g * GATHER_UNROLL + k
            row = idx_ref[idx_base + r]
            pltpu.make_async_copy(src_hbm.at[pl.ds(row, 1), :], dst_ref.at[pl.ds(r, 1), :], sem).start(
                priority=priority)
        return carry

    lax.fori_loop(0, n_rows // GATHER_UNROLL, group, 0)


def _row_gather_wait(src_hbm, dst_ref, sem, n_rows):
    def group(g, carry):
        for k in range(GATHER_UNROLL):
            r = g * GATHER_UNROLL + k
            pltpu.make_async_copy(src_hbm.at[pl.ds(0, 1), :], dst_ref.at[pl.ds(r, 1), :], sem).wait()
        return carry

    lax.fori_loop(0, n_rows // GATHER_UNROLL, group, 0)


def _gathered_moe_rows(s1_ref, s2_ref, ys_hbm, buf, sem, tm, off):
    i = pl.program_id(0)
    n = pl.num_programs(0)
    slot = lax.rem(i, 2)

    def start(step, into):
        base = (step + off) * tm
        _row_gather_start(ys_hbm, s1_ref, base, buf.at[into, 0], sem.at[into], tm, 0)
        _row_gather_start(ys_hbm, s2_ref, base, buf.at[into, 1], sem.at[into], tm, 1)

    @pl.when(i == 0)
    def _():
        start(0, 0)

    @pl.when(i + 1 < n)
    def _():
        start(i + 1, 1 - slot)

    _row_gather_wait(ys_hbm, buf.at[slot, 0], sem.at[slot], tm)
    _row_gather_wait(ys_hbm, buf.at[slot, 1], sem.at[slot], tm)
    return buf[slot, 0], buf[slot, 1]


def _unpack_expert_rows(w, n_split):
    wb = w.shape[1] // n_split
    parts = []
    for j in range(n_split):
        lo, hi = _unpack_halves(w[:, j * wb:(j + 1) * wb])
        parts += [lo, hi]
    return jnp.concatenate(parts, axis=1)


def _moe_residual(s1_ref, s2_ref, ys_hbm, buf, sem, tm, off, n_split, x_ref, w1_ref, w2_ref, g2_ref):
    p1, p2 = _gathered_moe_rows(s1_ref, s2_ref, ys_hbm, buf, sem, tm, off)
    moe = w1_ref[...] * _unpack_expert_rows(p1, n_split) + w2_ref[...] * _unpack_expert_rows(p2, n_split)
    return x_ref[...] + g2_ref[...] * moe


def _combine_norm_kernel(tm, n_split, s1_ref, s2_ref, x_ref, ys_hbm, w1_ref, w2_ref, g2_ref, g_ref, sc_ref, sh_ref,
                         xo_ref, u_ref, buf, sem):
    x = _moe_residual(s1_ref, s2_ref, ys_hbm, buf, sem, tm, 0, n_split, x_ref, w1_ref, w2_ref, g2_ref)
    xo_ref[...] = x
    u_ref[...] = _norm_mod(x, g_ref[...], sc_ref[...], sh_ref[...]).astype(u_ref.dtype)


def _gather_cparams():
    return pltpu.CompilerParams(dimension_semantics=("arbitrary",), vmem_limit_bytes=VMEM_LIMIT,
                                disable_bounds_checks=True)


def _combine_scratch(dm, tm):
    return [pltpu.VMEM((2, TOP_K, tm, dm.D // 2), jnp.uint32), pltpu.SemaphoreType.DMA((2,))]


def _combine_norm(dm, layer, x, ysp, slot1, slot2, w1, w2, g_next, mod4, n_split, tm=256):
    row = pl.BlockSpec((tm, dm.D), lambda i, s1, s2: (i, 0))
    col = pl.BlockSpec((tm, 1), lambda i, s1, s2: (i, 0))
    vec = pl.BlockSpec((1, dm.D), lambda i, s1, s2: (0, 0))

    def mod(lyr, section):
        return pl.BlockSpec((None, None, 1, dm.D), lambda i, s1, s2: (lyr, _cond_idx(dm, i, tm), 0, section))

    return pl.pallas_call(
        functools.partial(_combine_norm_kernel, tm, n_split),
        grid_spec=pltpu.PrefetchScalarGridSpec(
            num_scalar_prefetch=2, grid=(dm.T // tm,),
            in_specs=[row, pl.BlockSpec(memory_space=pl.ANY), col, col, mod(layer, 5), vec,
                      mod(layer + 1, 1), mod(layer + 1, 0)],
            out_specs=[row, row],
            scratch_shapes=_combine_scratch(dm, tm)),
        out_shape=[jax.ShapeDtypeStruct((dm.T, dm.D), F32), jax.ShapeDtypeStruct((dm.T, dm.D), BF16)],
        compiler_params=_gather_cparams(),
        name="combine_norm",
    )(slot1, slot2, x, ysp, w1, w2, mod4, g_next, mod4, mod4)


def _combine_final_kernel(tm, off, n_split, s1_ref, s2_ref, x_ref, ys_hbm, w1_ref, w2_ref, g2_ref, g_ref, o_ref,
                          buf, sem):
    x = _moe_residual(s1_ref, s2_ref, ys_hbm, buf, sem, tm, off, n_split, x_ref, w1_ref, w2_ref, g2_ref)
    ms = jnp.mean(x * x, axis=-1, keepdims=True)
    o_ref[...] = x * lax.rsqrt(ms + EPS) * g_ref[...]


def _combine_final(dm, layer, x, ysp, slot1, slot2, w1, w2, g_final, mod4, n_split, row_start, n_rows, tm=256):
    off = row_start // tm
    row = pl.BlockSpec((tm, dm.D), lambda i, s1, s2: (i + off, 0))
    col = pl.BlockSpec((tm, 1), lambda i, s1, s2: (i + off, 0))
    return pl.pallas_call(
        functools.partial(_combine_final_kernel, tm, off, n_split),
        grid_spec=pltpu.PrefetchScalarGridSpec(
            num_scalar_prefetch=2, grid=(n_rows // tm,),
            in_specs=[row, pl.BlockSpec(memory_space=pl.ANY), col, col,
                      pl.BlockSpec((None, None, 1, dm.D),
                                   lambda i, s1, s2: (layer, _cond_idx(dm, i + off, tm), 0, 5)),
                      pl.BlockSpec((1, dm.D), lambda i, s1, s2: (0, 0))],
            out_specs=pl.BlockSpec((tm, dm.D), lambda i, s1, s2: (i, 0)),
            scratch_shapes=_combine_scratch(dm, tm)),
        out_shape=jax.ShapeDtypeStruct((n_rows, dm.D), F32),
        compiler_params=_gather_cparams(),
        name="combine_final",
    )(slot1, slot2, x, ysp, w1, w2, mod4, g_final)


def _mm_kernel(a_ref, w_ref, o_ref):
    o_ref[...] = jnp.dot(a_ref[...], w_ref[...], preferred_element_type=F32).astype(o_ref.dtype)


def _matmul(a, w, out_dtype, tm, tn, name):
    m, k = a.shape
    n = w.shape[1]
    return pl.pallas_call(
        _mm_kernel,
        grid=(n // tn, m // tm),
        in_specs=[pl.BlockSpec((tm, k), lambda j, i: (i, 0)), pl.BlockSpec((k, tn), lambda j, i: (0, j))],
        out_specs=pl.BlockSpec((tm, tn), lambda j, i: (i, j)),
        out_shape=jax.ShapeDtypeStruct((m, n), out_dtype),
        compiler_params=_cparams(("arbitrary", "arbitrary")),
        name=name,
    )(a, w)


def _merge_kernel(ya_ref, yb_ref, yc_ref, g0_ref, g1_ref, g2_ref, wa_ref, wb_ref, wc_ref, o_ref):
    acc = _sigmoid(g0_ref[...].astype(F32)) * jnp.dot(ya_ref[...], wa_ref[...], preferred_element_type=F32)
    acc += _sigmoid(g1_ref[...].astype(F32)) * jnp.dot(yb_ref[...], wb_ref[...], preferred_element_type=F32)
    acc += _sigmoid(g2_ref[...].astype(F32)) * jnp.dot(yc_ref[...], wc_ref[...], preferred_element_type=F32)
    o_ref[...] = acc.astype(o_ref.dtype)


def _merge(dm, ya, yb, yc, proj, wa, wb, wc, tm=512, tn=1024):
    tn = min(tn, dm.D)
    nb = dm.D // tn
    gb = dm.o_gate // tn

    def gate_spec(k):
        return pl.BlockSpec((tm, tn), lambda j, i: (i, gb + k * nb + j))

    def a_spec(kd):
        return pl.BlockSpec((tm, kd), lambda j, i: (i, 0))

    def w_spec(kd):
        return pl.BlockSpec((kd, tn), lambda j, i: (0, j))

    return pl.pallas_call(
        _merge_kernel,
        grid=(nb, dm.T // tm),
        in_specs=[a_spec(dm.d_lru), a_spec(dm.d_ssd), a_spec(dm.d_pool), gate_spec(0), gate_spec(1), gate_spec(2),
                  w_spec(dm.d_lru), w_spec(dm.d_ssd), w_spec(dm.d_pool)],
        out_specs=pl.BlockSpec((tm, tn), lambda j, i: (i, j)),
        out_shape=jax.ShapeDtypeStruct((dm.T, dm.D), BF16),
        compiler_params=_cparams(("arbitrary", "arbitrary")),
        name="merge",
    )(ya, yb, yc, proj, proj, proj, wa, wb, wc)


def _outproj_kernel(a_ref, w_ref, x_ref, g_ref, o_ref):
    o_ref[...] = x_ref[...] + g_ref[...] * jnp.dot(a_ref[...], w_ref[...], preferred_element_type=F32)


def _outproj(dm, layer, merged, w_out, x, mod4, tm=512, tn=1024):
    tn = min(tn, dm.D)
    nb = dm.D // tn
    return pl.pallas_call(
        _outproj_kernel,
        grid=(nb, dm.T // tm),
        in_specs=[
            pl.BlockSpec((tm, dm.D), lambda j, i: (i, 0)),
            pl.BlockSpec((dm.D, tn), lambda j, i: (0, j)),
            pl.BlockSpec((tm, tn), lambda j, i: (i, j)),
            pl.BlockSpec((None, None, 1, tn), lambda j, i: (layer, _cond_idx(dm, i, tm), 0, 2 * nb + j)),
        ],
        out_specs=pl.BlockSpec((tm, tn), lambda j, i: (i, j)),
        out_shape=jax.ShapeDtypeStruct((dm.T, dm.D), F32),
        compiler_params=_cparams(("arbitrary", "arbitrary")),
        name="outproj",
    )(merged, w_out, x, mod4)


def _halo_specs(dm, tl, width, col_block, tile_of):
    per = tl // HALO
    last = dm.T // HALO - 1
    cur = pl.BlockSpec((tl, width), lambda i: (tile_of(i), col_block))
    prev = pl.BlockSpec((HALO, width), lambda i: (jnp.maximum(tile_of(i) * per - 1, 0), col_block))
    nxt = pl.BlockSpec((HALO, width), lambda i: (jnp.minimum((tile_of(i) + 1) * per, last), col_block))
    return [cur, prev, nxt]


def _fill_ext(ext_ref, cur_ref, prev_ref, next_ref, is_first, is_last, tl):
    ext_ref[pl.ds(HALO, tl), :] = cur_ref[...].astype(F32)
    ext_ref[pl.ds(0, HALO), :] = jnp.where(is_first, 0.0, prev_ref[...].astype(F32))
    ext_ref[pl.ds(HALO + tl, HALO), :] = jnp.where(is_last, 0.0, next_ref[...].astype(F32))


def _conv_from_ext(ext_ref, w_ref, b_ref, tl):
    y = b_ref[...] + ext_ref[pl.ds(HALO, tl), :] * w_ref[CONV_LEFT:CONV_LEFT + 1, :]
    for k in range(CONV_W):
        if k != CONV_LEFT:
            y = y + ext_ref[pl.ds(HALO - CONV_LEFT + k, tl), :] * w_ref[k:k + 1, :]
    return y


def _lru_kernel(dm, tl, reverse, *refs):
    if reverse:
        (xa_ref, xp_ref, xn_ref, cw_ref, cb_ref, wr_ref, br_ref, wi_ref, bi_ref, lam_ref, h0_ref,
         ga_ref, hf_ref, y_ref, hT_ref, ext_ref, a_ref, bx_ref, hs_ref, h_ref) = refs
    else:
        (xa_ref, xp_ref, xn_ref, cw_ref, cb_ref, wr_ref, br_ref, wi_ref, bi_ref, lam_ref, h0_ref,
         y_ref, hT_ref, ext_ref, a_ref, bx_ref, hs_ref, h_ref) = refs
    nt = dm.T // tl
    i = pl.program_id(0)
    tile = nt - 1 - i if reverse else i
    pos, lseq, _ = _seq_tile_info(dm, tile, tl)
    is_first = pos == 0
    is_last = pos + tl == lseq
    _fill_ext(ext_ref, xa_ref, xp_ref, xn_ref, is_first, is_last, tl)
    xc = _conv_from_ext(ext_ref, cw_ref, cb_ref, tl)
    sp = _softplus(-lam_ref[...])
    bw = dm.lru_bw
    for n in range(dm.lru_blocks):
        cs = slice(n * bw, (n + 1) * bw)
        xb = xc[:, cs]
        xbh = xb.astype(BF16)
        r = _sigmoid(jnp.dot(xbh, wr_ref[n], preferred_element_type=F32) + br_ref[:, cs])
        ig = _sigmoid(jnp.dot(xbh, wi_ref[n], preferred_element_type=F32) + bi_ref[:, cs])
        log_a = (-LRU_C) * r * sp[:, cs]
        a = jnp.exp(log_a)
        a_ref[:, cs] = a
        bx_ref[:, cs] = jnp.sqrt(1.0 - a * a) * (ig * xb)

    start_of_scan = is_last if reverse else is_first

    @pl.when(start_of_scan)
    def _():
        h_ref[...] = h0_ref[...]

    def body(t, h):
        tt = tl - 1 - t if reverse else t
        h = a_ref[pl.ds(tt, 1), :] * h + bx_ref[pl.ds(tt, 1), :]
        hs_ref[pl.ds(tt, 1), :] = h
        return h

    h = lax.fori_loop(0, tl, body, h_ref[...], unroll=8)
    h_ref[...] = h
    hT_ref[...] = h
    if reverse:
        y = (hf_ref[...].astype(F32) + hs_ref[...]) * _gelu_tanh(ga_ref[...].astype(F32))
        y_ref[...] = y.astype(y_ref.dtype)
    else:
        y_ref[...] = hs_ref[...].astype(y_ref.dtype)


def _lru(dm, reverse, proj, cw, cb, wr, br, wi, bi, lam, h0, hs_fwd=None, tl=256):
    nt = dm.T // tl
    c = dm.d_lru
    tile_of = (lambda i: nt - 1 - i) if reverse else (lambda i: i)
    seq_of = lambda i: _seq_tile_info(dm, tile_of(i), tl)[2]
    full = lambda shape: pl.BlockSpec(shape, lambda i: (0,) * len(shape))
    row = pl.BlockSpec((tl, c), lambda i: (tile_of(i), 0))
    in_specs = _halo_specs(dm, tl, c, dm.o_xa // c, tile_of) + [
        full((CONV_W, c)), full((1, c)),
        full(wr.shape), full((1, c)), full(wi.shape), full((1, c)), full((1, c)),
        pl.BlockSpec((None, 1, c), lambda i: (seq_of(i), 0, 0)),
    ]
    args = [proj, proj, proj, cw, cb, wr, br, wi, bi, lam, h0]
    if reverse:
        in_specs += [pl.BlockSpec((tl, c), lambda i: (tile_of(i), dm.o_ga // c)), row]
        args += [proj, hs_fwd]
    return pl.pallas_call(
        functools.partial(_lru_kernel, dm, tl, reverse),
        grid=(nt,),
        in_specs=in_specs,
        out_specs=[row, pl.BlockSpec((None, 1, c), lambda i: (seq_of(i), 0, 0))],
        out_shape=[jax.ShapeDtypeStruct((dm.T, c), BF16), jax.ShapeDtypeStruct((dm.nseq, 1, c), F32)],
        scratch_shapes=[pltpu.VMEM((tl + 2 * HALO, c), F32), pltpu.VMEM((tl, c), F32), pltpu.VMEM((tl, c), F32),
                        pltpu.VMEM((tl, c), F32), pltpu.VMEM((1, c), F32)],
        compiler_params=_cparams(("arbitrary",)),
        name="lru_bwd" if reverse else "lru_fwd",
    )(*args)


def _pool_kernel(dm, tl, xc_ref, xp_ref, xn_ref, w_ref, s_ref, o_ref, ext_ref):
    tile = pl.program_id(0)
    pos, lseq, _ = _seq_tile_info(dm, tile, tl)
    _fill_ext(ext_ref, xc_ref, xp_ref, xn_ref, pos == 0, pos + tl == lseq, tl)
    t = pos + lax.broadcasted_iota(jnp.int32, (tl, 1), 0)
    pg = dm.pool_group
    for k, w in enumerate(POOL_WINDOWS):
        cs = slice(k * pg, (k + 1) * pg)
        tok_rows = ext_ref[pl.ds(HALO, tl), cs]
        acc = tok_rows
        for o in range(-w // 2, w // 2):
            if o != 0:
                acc = acc + ext_ref[pl.ds(HALO + o, tl), cs]
        cnt = (jnp.minimum(t + w // 2, lseq) - jnp.maximum(t - w // 2, 0)).astype(F32)
        dev = acc / cnt - tok_rows
        y = jnp.dot(dev.astype(BF16), w_ref[k], preferred_element_type=F32)
        o_ref[:, cs] = (y * s_ref[:, cs]).astype(o_ref.dtype)


def _pool(dm, proj, w_pool, scale, tl=256):
    c = dm.d_pool
    return pl.pallas_call(
        functools.partial(_pool_kernel, dm, tl),
        grid=(dm.T // tl,),
        in_specs=_halo_specs(dm, tl, c, dm.o_xc // c, lambda i: i) + [
            pl.BlockSpec(w_pool.shape, lambda i: (0, 0, 0)), pl.BlockSpec((1, c), lambda i: (0, 0))],
        out_specs=pl.BlockSpec((tl, c), lambda i: (i, 0)),
        out_shape=jax.ShapeDtypeStruct((dm.T, c), BF16),
        scratch_shapes=[pltpu.VMEM((tl + 2 * HALO, c), F32)],
        compiler_params=_cparams(("arbitrary",)),
        name="pool",
    )(proj, proj, proj, w_pool, scale)


def _dot_exact(a, b):
    return jnp.dot(a, b, preferred_element_type=F32, precision=lax.Precision.HIGHEST)


def _ssd_kernel(dm, q, reverse, *refs):
    if reverse:
        (xact_ref, dt_ref, dtb_ref, alog_ref, h0_ref, z_ref, yf_ref, ng_ref,
         y_ref, hT_ref, h_ref, yacc_ref) = refs
    else:
        (xs_ref, xsp_ref, xsn_ref, bc_ref, bcp_ref, bcn_ref, dt_ref, cwx_ref, cbx_ref, cwb_ref, cbb_ref,
         dtb_ref, alog_ref, h0_ref, dsk_ref, y_ref, hT_ref, xact_ref, extx_ref, extb_ref, h_ref) = refs
    nt = dm.T // q
    i = pl.program_id(0)
    tile = nt - 1 - i if reverse else i
    pos, lseq, _ = _seq_tile_info(dm, tile, q)
    is_first = pos == 0
    is_last = pos + q == lseq
    d_ssd, n, g_cnt, r_cnt, p = dm.d_ssd, dm.N, dm.G, dm.R, dm.P
    gn = g_cnt * n
    pair = 2 * p
    assert pair == LANES and n == LANES and q == LANES
    if reverse:
        xs = xact_ref[:, :d_ssd].astype(F32)
        bc = None
    else:
        _fill_ext(extx_ref, xs_ref, xsp_ref, xsn_ref, is_first, is_last, q)
        _fill_ext(extb_ref, bc_ref, bcp_ref, bcn_ref, is_first, is_last, q)
        xs = _silu(_conv_from_ext(extx_ref, cwx_ref, cbx_ref, q))
        bc = _silu(_conv_from_ext(extb_ref, cwb_ref, cbb_ref, q))
        xact_ref[:, :d_ssd] = xs.astype(xact_ref.dtype)
        xact_ref[:, d_ssd:] = bc.astype(xact_ref.dtype)

    @pl.when(is_last if reverse else is_first)
    def _():
        h_ref[...] = h0_ref[...]

    dt = _softplus(dt_ref[...] + dtb_ref[...])
    a_neg = -jnp.exp(alog_ref[...])
    dta = dt * a_neg
    ri = lax.broadcasted_iota(jnp.int32, (q, q), 0)
    ci = lax.broadcasted_iota(jnp.int32, (q, q), 1)
    valid = (ci >= ri) if reverse else (ci <= ri)
    a_cs = _dot_exact(valid.astype(F32), dta)
    a_end = a_cs[0:1, :] if reverse else a_cs[q - 1:q, :]
    w_state = dt * jnp.exp(a_end - a_cs)
    src_t = (a_cs - jnp.log(dt)).T
    cd_t = jnp.broadcast_to(jnp.exp(a_end), (q, LANES)).T
    lane = lax.broadcasted_iota(jnp.int32, (q, LANES), 1)
    low_half = lane < p

    for g in range(g_cnt):
        if reverse:
            b_h = xact_ref[:, d_ssd + g * n:d_ssd + (g + 1) * n]
            c_h = xact_ref[:, d_ssd + gn + g * n:d_ssd + gn + (g + 1) * n]
        else:
            b_h = bc[:, g * n:(g + 1) * n].astype(BF16)
            c_h = bc[:, gn + g * n:gn + (g + 1) * n].astype(BF16)
        scores = lax.dot_general(c_h, b_h, (((1,), (1,)), ((), ())), preferred_element_type=F32)
        r0 = g * r_cnt * p
        h_grp = h_ref[pl.ds(r0, r_cnt * p), :].astype(BF16)
        y_off = lax.dot_general(c_h, h_grp, (((1,), (1,)), ((), ())), preferred_element_type=F32)
        for k in range(r_cnt // 2):
            c0 = r0 + 2 * k * p
            hd0 = g * r_cnt + 2 * k
            xs_pair = xs[:, c0:c0 + pair]
            xs_h = xact_ref[:, c0:c0 + pair] if reverse else xs_pair.astype(BF16)
            ys, cols, ws = [], [], []
            for s in range(2):
                hd = hd0 + s
                col = jnp.broadcast_to(a_cs[:, hd:hd + 1], (q, q))
                row = jnp.broadcast_to(src_t[hd:hd + 1, :], (q, q))
                m = (scores * jnp.exp(jnp.where(valid, col - row, -jnp.inf))).astype(BF16)
                ys.append(jnp.dot(m, xs_h, preferred_element_type=F32))
                cols.append(col)
                ws.append(jnp.broadcast_to(w_state[:, hd:hd + 1], (q, LANES)))
            y_pair = (jnp.where(low_half, ys[0], ys[1])
                      + y_off[:, c0 - r0:c0 - r0 + pair] * jnp.exp(jnp.where(low_half, cols[0], cols[1])))
            xsw = (xs_pair * jnp.where(low_half, ws[0], ws[1])).astype(BF16)
            st = lax.dot_general(xsw, b_h, (((0,), (0,)), ((), ())), preferred_element_type=F32)
            dec = jnp.concatenate([jnp.broadcast_to(cd_t[hd0:hd0 + 1, :], (p, n)),
                                   jnp.broadcast_to(cd_t[hd0 + 1:hd0 + 2, :], (p, n))], axis=0)
            h_ref[pl.ds(c0, pair), :] = dec * h_ref[pl.ds(c0, pair), :] + st
            if reverse:
                yacc_ref[:, c0:c0 + pair] = y_pair
            else:
                y_ref[:, c0:c0 + pair] = (y_pair + dsk_ref[:, c0:c0 + pair] * xs_pair).astype(y_ref.dtype)

    hT_ref[...] = h_ref[...]
    if reverse:
        gw = d_ssd // g_cnt
        for g in range(g_cnt):
            cs = slice(g * gw, (g + 1) * gw)
            y = (yacc_ref[:, cs] + yf_ref[:, cs]) * _silu(z_ref[:, cs].astype(F32))
            ms = jnp.mean(y * y, axis=-1, keepdims=True)
            y_ref[:, cs] = (y * lax.rsqrt(ms + EPS) * ng_ref[:, cs]).astype(y_ref.dtype)


def _ssd(dm, reverse, proj, dtp, cw, cb, dtb, alog, h0, extra, fwd=None, q=128):
    nt = dm.T // q
    tile_of = (lambda i: nt - 1 - i) if reverse else (lambda i: i)
    seq_of = lambda i: _seq_tile_info(dm, tile_of(i), q)[2]
    full = lambda shape: pl.BlockSpec(shape, lambda i: (0,) * len(shape))
    hp = dm.H * dm.P
    rowy = pl.BlockSpec((q, dm.d_ssd), lambda i: (tile_of(i), 0))
    rowact = pl.BlockSpec((q, dm.d_xbc), lambda i: (tile_of(i), 0))
    state = pl.BlockSpec((None, hp, dm.N), lambda i: (seq_of(i), 0, 0))
    dt_spec = pl.BlockSpec((q, LANES), lambda i: (tile_of(i), 1 if reverse else 0))
    vec = [full((1, LANES)), full((1, LANES)), state]
    if reverse:
        y_fwd, xact = fwd
        in_specs = [rowact, dt_spec] + vec + [
            pl.BlockSpec((q, dm.d_ssd), lambda i: (tile_of(i), dm.o_z // dm.d_ssd)), rowy, full((1, dm.d_ssd))]
        args = [xact, dtp, dtb, alog, h0, proj, y_fwd, extra]
        out_specs = [rowy, state]
        out_shape = [jax.ShapeDtypeStruct((dm.T, dm.d_ssd), BF16), jax.ShapeDtypeStruct((dm.nseq, hp, dm.N), F32)]
        scratch = [pltpu.VMEM((hp, dm.N), F32), pltpu.VMEM((q, dm.d_ssd), F32)]
    else:
        in_specs = (_halo_specs(dm, q, dm.d_ssd, dm.o_xs // dm.d_ssd, tile_of)
                    + _halo_specs(dm, q, dm.d_bc, dm.o_bc // dm.d_bc, tile_of)
                    + [dt_spec, full((CONV_W, dm.d_ssd)), full((1, dm.d_ssd)), full((CONV_W, dm.d_bc)),
                       full((1, dm.d_bc))] + vec + [full((1, dm.d_ssd))])
        args = [proj] * 6 + [dtp, cw[:, :dm.d_ssd], cb[:, :dm.d_ssd], cw[:, dm.d_ssd:], cb[:, dm.d_ssd:],
                             dtb, alog, h0, extra]
        out_specs = [rowy, state, rowact]
        out_shape = [jax.ShapeDtypeStruct((dm.T, dm.d_ssd), F32), jax.ShapeDtypeStruct((dm.nseq, hp, dm.N), F32),
                     jax.ShapeDtypeStruct((dm.T, dm.d_xbc), BF16)]
        scratch = [pltpu.VMEM((q + 2 * HALO, dm.d_ssd), F32), pltpu.VMEM((q + 2 * HALO, dm.d_bc), F32),
                   pltpu.VMEM((hp, dm.N), F32)]
    return pl.pallas_call(
        functools.partial(_ssd_kernel, dm, q, reverse),
        grid=(nt,),
        in_specs=in_specs,
        out_specs=out_specs,
        out_shape=out_shape,
        scratch_shapes=scratch,
        compiler_params=_cparams(("arbitrary",)),
        name="ssd_bwd" if reverse else "ssd_fwd",
    )(*args)


def _moe_dispatch_kernel(tm, n_exp, s1_ref, s2_ref, lo_ref, hi_ref, u_ref, o_hbm, zrow, sem, zsem):
    i = pl.program_id(0)
    base = i * tm

    def start_group(g, carry):
        for k in range(GATHER_UNROLL):
            r = g * GATHER_UNROLL + k
            src = u_ref.at[pl.ds(r, 1), :]
            pltpu.make_async_copy(src, o_hbm.at[pl.ds(s1_ref[base + r], 1), :], sem).start(priority=0)
            pltpu.make_async_copy(src, o_hbm.at[pl.ds(s2_ref[base + r], 1), :], sem).start(priority=1)
        return carry

    def wait_group(g, carry):
        for k in range(GATHER_UNROLL):
            r = g * GATHER_UNROLL + k
            src = u_ref.at[pl.ds(r, 1), :]
            pltpu.make_async_copy(src, o_hbm.at[pl.ds(0, 1), :], sem).wait()
            pltpu.make_async_copy(src, o_hbm.at[pl.ds(0, 1), :], sem).wait()
        return carry

    lax.fori_loop(0, tm // GATHER_UNROLL, start_group, 0)
    lax.fori_loop(0, tm // GATHER_UNROLL, wait_group, 0)

    @pl.when(i == pl.num_programs(0) - 1)
    def _():
        zrow[...] = jnp.zeros_like(zrow)
        for e in range(n_exp):
            lo = lo_ref[e]
            hi = hi_ref[e]

            def zstart(r, carry):
                pltpu.make_async_copy(zrow, o_hbm.at[pl.ds(r, 1), :], zsem).start()
                return carry

            def zwait(r, carry):
                pltpu.make_async_copy(zrow, o_hbm.at[pl.ds(0, 1), :], zsem).wait()
                return carry

            lax.fori_loop(lo, hi, zstart, 0)
            lax.fori_loop(lo, hi, zwait, 0)


def _moe_dispatch(u2p, slot1, slot2, pad_lo, pad_hi, p_pad, tm):
    t, words = u2p.shape
    n_exp = pad_lo.shape[0]
    return pl.pallas_call(
        functools.partial(_moe_dispatch_kernel, tm, n_exp),
        grid_spec=pltpu.PrefetchScalarGridSpec(
            num_scalar_prefetch=4, grid=(t // tm,),
            in_specs=[pl.BlockSpec((tm, words), lambda i, s1, s2, lo, hi: (i, 0))],
            out_specs=pl.BlockSpec(memory_space=pl.ANY),
            scratch_shapes=[pltpu.VMEM((1, words), jnp.uint32), pltpu.SemaphoreType.DMA(()),
                            pltpu.SemaphoreType.DMA(())]),
        out_shape=jax.ShapeDtypeStruct((p_pad, words), jnp.uint32),
        compiler_params=_gather_cparams(),
        name="moe_dispatch",
    )(slot1, slot2, pad_lo, pad_hi, u2p)


def _expert_changed(te_ref, i):
    return jnp.logical_or(i == 0, te_ref[i] != te_ref[jnp.maximum(i - 1, 0)])


def _expert_up_kernel(te_ref, nu_ref, x_ref, wg_ref, wu_ref, h_ref, wg_s, wu_s):
    i = pl.program_id(1)

    @pl.when(jnp.logical_and(i < nu_ref[0], _expert_changed(te_ref, i)))
    def _():
        wg_s[...] = wg_ref[...].astype(BF16)
        wu_s[...] = wu_ref[...].astype(BF16)

    @pl.when(i < nu_ref[0])
    def _():
        lo, hi = _unpack_halves(x_ref[...])
        lo = lo.astype(BF16)
        hi = hi.astype(BF16)
        half = lo.shape[1]
        a = (jnp.dot(lo, wg_s[:half, :], preferred_element_type=F32)
             + jnp.dot(hi, wg_s[half:, :], preferred_element_type=F32))
        b = (jnp.dot(lo, wu_s[:half, :], preferred_element_type=F32)
             + jnp.dot(hi, wu_s[half:, :], preferred_element_type=F32))
        h_ref[...] = (_silu(a) * b).astype(h_ref.dtype)

    @pl.when(i >= nu_ref[0])
    def _():
        h_ref[...] = jnp.zeros_like(h_ref)


def _expert_down_kernel(te_ref, nu_ref, h_ref, wd_ref, y_ref, wd_s):
    i = pl.program_id(1)

    @pl.when(jnp.logical_and(i < nu_ref[0], _expert_changed(te_ref, i)))
    def _():
        wd_s[...] = wd_ref[...].astype(BF16)

    @pl.when(i < nu_ref[0])
    def _():
        y_ref[...] = _pack_halves(jnp.dot(h_ref[...], wd_s[...], preferred_element_type=F32))

    @pl.when(i >= nu_ref[0])
    def _():
        y_ref[...] = jnp.zeros_like(y_ref)


def _experts(dm, layer, xsp, w_gate, w_up, w_down, tile_expert, n_used, tm, up_split, down_split):
    p_pad = xsp.shape[0]
    n_tiles = p_pad // tm
    d, f = dm.D, dm.F
    fh, dh = f // up_split, d // down_split
    h = pl.pallas_call(
        _expert_up_kernel,
        grid_spec=pltpu.PrefetchScalarGridSpec(
            num_scalar_prefetch=2, grid=(up_split, n_tiles),
            in_specs=[pl.BlockSpec((tm, d // 2), lambda j, i, te, nu: (i, 0)),
                      pl.BlockSpec((None, None, d, fh), lambda j, i, te, nu: (layer, te[i], 0, j)),
                      pl.BlockSpec((None, None, d, fh), lambda j, i, te, nu: (layer, te[i], 0, j))],
            out_specs=pl.BlockSpec((tm, fh), lambda j, i, te, nu: (i, j)),
            scratch_shapes=[pltpu.VMEM((d, fh), BF16), pltpu.VMEM((d, fh), BF16)]),
        out_shape=jax.ShapeDtypeStruct((p_pad, f), BF16),
        compiler_params=_cparams(("arbitrary", "arbitrary")),
        name="expert_up",
    )(tile_expert, n_used, xsp, w_gate, w_up)
    return pl.pallas_call(
        _expert_down_kernel,
        grid_spec=pltpu.PrefetchScalarGridSpec(
            num_scalar_prefetch=2, grid=(down_split, n_tiles),
            in_specs=[pl.BlockSpec((tm, f), lambda j, i, te, nu: (i, 0)),
                      pl.BlockSpec((None, None, f, dh), lambda j, i, te, nu: (layer, te[i], 0, j))],
            out_specs=pl.BlockSpec((tm, dh // 2), lambda j, i, te, nu: (i, j)),
            scratch_shapes=[pltpu.VMEM((f, dh), BF16)]),
        out_shape=jax.ShapeDtypeStruct((p_pad, d // 2), jnp.uint32),
        compiler_params=_cparams(("arbitrary", "arbitrary")),
        name="expert_down",
    )(tile_expert, n_used, h, w_down)


def _dispatch_plan(dm, route, tm):
    t = dm.T
    e_cnt = dm.E
    ids = jnp.arange(e_cnt, dtype=jnp.int32)[None, :]
    first = route[0].astype(jnp.int32)[:, None] == ids
    second = route[1].astype(jnp.int32)[:, None] == ids
    uses = jnp.logical_or(first, second).astype(jnp.int32)
    before = jnp.cumsum(uses, axis=0) - uses
    counts = jnp.sum(uses, axis=0)
    tiles_per = (counts + tm - 1) // tm
    tile_end = jnp.cumsum(tiles_per)
    start = (tile_end - tiles_per) * tm
    row = start[None, :] + before
    slot1 = jnp.sum(jnp.where(first, row, 0), axis=1)
    slot2 = jnp.sum(jnp.where(second, row, 0), axis=1)
    n_tiles = (TOP_K * t) // tm + e_cnt
    p_pad = n_tiles * tm
    tile_ids = jnp.arange(n_tiles, dtype=jnp.int32)
    tile_expert = jnp.minimum(jnp.sum((tile_end[None, :] <= tile_ids[:, None]).astype(jnp.int32), axis=1),
                              e_cnt - 1)
    n_used = tile_end[-1:].astype(jnp.int32)
    pad_lo = jnp.concatenate([start + counts, tile_end[-1:] * tm]).astype(jnp.int32)
    pad_hi = jnp.concatenate([tile_end * tm, jnp.full((1,), p_pad, tile_end.dtype)]).astype(jnp.int32)
    return p_pad, pad_lo, pad_hi, tile_expert, n_used, slot1, slot2


def _grid_pos_embed(n_tokens, d_model):
    rows = n_tokens // GRID_W
    row = jnp.repeat(jnp.arange(rows), GRID_W).astype(F32)
    col = jnp.tile(jnp.arange(GRID_W), rows).astype(F32)
    quarter = d_model // 4
    omega = 1.0 / (10000.0 ** (jnp.arange(quarter, dtype=F32) / quarter))
    ang_r = row[:, None] * omega[None, :]
    ang_c = col[:, None] * omega[None, :]
    return jnp.concatenate([jnp.sin(ang_r), jnp.cos(ang_r), jnp.sin(ang_c), jnp.cos(ang_c)], axis=-1)


def _pad_lanes(v, fill=0.0):
    return jnp.pad(v.astype(F32), (0, LANES - v.shape[0]), constant_values=fill).reshape(1, LANES)


def kernel(x_prompt, x_sample, state_lru, state_ssd, c, c_ctx, w_mod, b_mod, norm1_g, w_in, conv_a_w, conv_a_b, w_r, b_r, w_i, b_i, lru_lambda, w_oa, conv_b_w, conv_b_b, dt_bias, a_log, d_skip, ssd_norm_g, w_ob, w_pool, pool_scale, w_oc, w_out, norm2_g, w_router, b_router, w_gate, w_up, w_down, final_g):
    dm = Dims(x_prompt, x_sample, state_ssd, w_r, w_pool, w_gate, dt_bias, conv_b_w, w_oa, w_ob, w_oc)
    d = dm.D
    moe_tm = min(512, dm.Lp)

    ncond = 16
    cond = jnp.zeros((ncond, d), F32).at[:dm.Bs].set(c).at[dm.Bs].set(c_ctx)
    mod4 = _mod_table(cond, w_mod, b_mod).reshape(dm.depth, ncond, 1, 6 * d)

    pos = _grid_pos_embed(dm.Ls, d)
    x, u = _embed_norm(dm, x_sample.reshape(dm.Ts, d), x_prompt.reshape(dm.Tp, d), pos,
                       norm1_g[0].reshape(1, d), mod4)

    wr_pad = jnp.pad(w_router, ((0, 0), (0, LANES - dm.E)))
    br_col = b_router.reshape(dm.E, 1).astype(F32)
    hp = dm.H * dm.P
    lru_states, ssd_states = [], []
    out = None
    for l in range(dm.depth):
        wl = w_in[l]
        w_main = jnp.concatenate([wl[:, :dm.o_dt], wl[:, dm.o_dt + 2 * dm.H:]], axis=1).astype(BF16)
        lane_pad = ((0, 0), (0, LANES - dm.H))
        w_dt = jnp.concatenate([jnp.pad(wl[:, dm.o_dt:dm.o_dt + dm.H], lane_pad),
                                jnp.pad(wl[:, dm.o_dt + dm.H:dm.o_dt + 2 * dm.H], lane_pad)], axis=1).astype(BF16)
        proj = _matmul(u, w_main, BF16, min(1024, dm.Lp), min(1024, dm.d_lru), "in_proj")
        dtp = _matmul(u, w_dt, F32, min(1024, dm.Lp), 2 * LANES, "dt_proj")

        h0_lru = [jnp.concatenate([state_lru[:, l, dd], jnp.zeros((dm.Bp, dm.d_lru), F32)], axis=0)
                  .reshape(dm.nseq, 1, dm.d_lru) for dd in range(2)]
        lru_args = lambda dd: (conv_a_w[l], conv_a_b[l].reshape(1, -1), w_r[l, dd].astype(BF16),
                               b_r[l, dd].reshape(1, -1), w_i[l, dd].astype(BF16), b_i[l, dd].reshape(1, -1),
                               lru_lambda[l, dd].reshape(1, -1), h0_lru[dd])
        hs_f, lru_tf = _lru(dm, False, proj, *lru_args(0))
        ya, lru_tb = _lru(dm, True, proj, *lru_args(1), hs_fwd=hs_f)
        lru_states.append(jnp.stack([lru_tf[dm.Bs:, 0], lru_tb[dm.Bs:, 0]], axis=1))

        h0_ssd = [jnp.concatenate([state_ssd[:, l, dd].reshape(dm.Bs, hp, dm.N),
                                   jnp.zeros((dm.Bp, hp, dm.N), F32)], axis=0) for dd in range(2)]
        ssd_args = lambda dd: (conv_b_w[l], conv_b_b[l].reshape(1, -1), _pad_lanes(dt_bias[l, dd]),
                               _pad_lanes(a_log[l, dd]), h0_ssd[dd])
        dsk = jnp.repeat(d_skip[l].astype(F32), dm.P).reshape(1, dm.d_ssd)
        y_f, ssd_tf, xact = _ssd(dm, False, proj, dtp, *ssd_args(0), dsk)
        yb, ssd_tb = _ssd(dm, True, proj, dtp, *ssd_args(1), ssd_norm_g[l].reshape(1, -1), fwd=(y_f, xact))
        ssd_states.append(jnp.stack([ssd_tf[dm.Bs:], ssd_tb[dm.Bs:]], axis=1)
                          .reshape(dm.Bp, 2, dm.H, dm.P, dm.N))

        yc = _pool(dm, proj, w_pool[l].astype(BF16), pool_scale[l].reshape(1, -1))

        merged = _merge(dm, ya, yb, yc, proj, w_oa[l].astype(BF16), w_ob[l].astype(BF16),
                        w_oc[l].astype(BF16), tm=min(512, dm.Lp))
        x = _outproj(dm, l, merged, w_out[l].astype(BF16), x, mod4, tm=min(512, dm.Lp))

        u2p, route = _norm_router(dm, l, x, norm2_g[l].reshape(1, d), mod4, wr_pad, br_col)
        p_pad, pad_lo, pad_hi, tile_expert, n_used, slot1, slot2 = _dispatch_plan(dm, route, moe_tm)
        n_split = 1
        xsp = _moe_dispatch(u2p, slot1, slot2, pad_lo, pad_hi, p_pad, moe_tm)
        ysp = _experts(dm, l, xsp, w_gate, w_up, w_down, tile_expert, n_used, moe_tm, 2, n_split)
        w1 = route[2].reshape(dm.T, 1)
        w2 = route[3].reshape(dm.T, 1)
        if l + 1 < dm.depth:
            x, u = _combine_norm(dm, l, x, ysp, slot1, slot2, w1, w2, norm1_g[l + 1].reshape(1, d), mod4, n_split)
        else:
            fg = final_g.reshape(1, d)
            out = (_combine_final(dm, l, x, ysp, slot1, slot2, w1, w2, fg, mod4, n_split, 0, dm.Ts),
                   _combine_final(dm, l, x, ysp, slot1, slot2, w1, w2, fg, mod4, n_split, dm.Ts, dm.Tp))

    y_sample = out[0].reshape(dm.Bs, dm.Ls, d)
    y_prompt = out[1].reshape(dm.Bp, dm.Lp, d)
    new_state_lru = jnp.stack(lru_states, axis=1).astype(x_prompt.dtype)
    new_state_ssd = jnp.stack(ssd_states, axis=1).astype(x_prompt.dtype)
    return (y_prompt, y_sample, new_state_lru, new_state_ssd)
```

```python
import functools
import math

import numpy as np
import jax
import jax.numpy as jnp
from jax import lax
from jax.experimental import pallas as pl
from jax.experimental.pallas import tpu as pltpu

F32 = jnp.float32
BF16 = jnp.bfloat16

EPS = 1e-6
GRID_W = 64
CONV_W = 4
CONV_LEFT = 1
LRU_C = 8.0
POOL_WINDOWS = (2, 4, 8, 16)
N_EXPERT_GROUPS = 4
TOP_K = 2

LANES = 128
HALO = 16
VMEM_LIMIT = 56 * 1024 * 1024


def _cparams(sem):
    return pltpu.CompilerParams(dimension_semantics=sem, vmem_limit_bytes=VMEM_LIMIT)


def _sigmoid_exp(x):
    return 1.0 / (1.0 + jnp.exp(-x))


def _sigmoid(x):
    return 0.5 * jnp.tanh(0.5 * x) + 0.5


def _silu(x):
    return x * _sigmoid(x)


def _softplus(x):
    return jnp.maximum(x, 0.0) + jnp.log(1.0 + jnp.exp(-jnp.abs(x)))


def _gelu_tanh(x):
    return x * (0.5 * (1.0 + jnp.tanh(math.sqrt(2.0 / math.pi) * (x + 0.044715 * (x * x * x)))))


def _pack_halves(x):
    c = x.shape[1] // 2
    lo = lax.bitcast_convert_type(x[:, :c].astype(BF16).astype(F32), jnp.uint32)
    hi = lax.bitcast_convert_type(x[:, c:].astype(BF16).astype(F32), jnp.uint32)
    return (hi & jnp.uint32(0xFFFF0000)) | (lo >> 16)


def _unpack_halves(w):
    lo = lax.bitcast_convert_type(w << 16, F32)
    hi = lax.bitcast_convert_type(w & jnp.uint32(0xFFFF0000), F32)
    return lo, hi


class Dims:
    def __init__(self, x_prompt, x_sample, state_ssd, w_r, w_pool, w_gate, dt_bias, conv_b_w, w_oa, w_ob, w_oc):
        self.Bp, self.Lp, self.D = x_prompt.shape
        self.Bs, self.Ls, _ = x_sample.shape
        self.depth = w_r.shape[0]
        self.Ts = self.Bs * self.Ls
        self.Tp = self.Bp * self.Lp
        self.T = self.Ts + self.Tp
        self.nseq = self.Bs + self.Bp
        self.d_lru = w_oa.shape[1]
        self.lru_blocks = w_r.shape[2]
        self.lru_bw = w_r.shape[3]
        self.d_ssd = w_ob.shape[1]
        self.H = dt_bias.shape[2]
        self.P = self.d_ssd // self.H
        self.N = state_ssd.shape[-1]
        self.d_xbc = conv_b_w.shape[2]
        self.G = (self.d_xbc - self.d_ssd) // (2 * self.N)
        self.R = self.H // self.G
        self.d_pool = w_oc.shape[1]
        self.pool_group = w_pool.shape[2]
        self.E = w_gate.shape[1]
        self.F = w_gate.shape[3]
        self.d_bc = 2 * self.G * self.N
        self.o_xa = 0
        self.o_ga = self.o_xa + self.d_lru
        self.o_z = self.o_ga + self.d_lru
        self.o_xs = self.o_z + self.d_ssd
        self.o_bc = self.o_xs + self.d_ssd
        self.o_dt = self.o_bc + self.d_bc
        self.o_xc = self.o_bc + self.d_bc
        self.o_gate = self.o_xc + self.d_pool
        self.n_main = self.o_gate + 3 * self.D


def _seq_tile_info(dm, tile, tl):
    row0 = tile * tl
    in_sample = row0 < dm.Ts
    rp = jnp.maximum(row0 - dm.Ts, 0)
    pos = jnp.where(in_sample, row0 % dm.Ls, rp % dm.Lp)
    lseq = jnp.where(in_sample, dm.Ls, dm.Lp)
    seq = jnp.where(in_sample, row0 // dm.Ls, dm.Bs + rp // dm.Lp)
    return pos, lseq, seq


def _cond_idx(dm, i, tm):
    return jnp.minimum((i * tm) // dm.Ls, dm.Bs)


def _mod_kernel(c_ref, w_ref, b_ref, o_ref):
    c = c_ref[...]
    a = _silu(c).astype(BF16)
    o_ref[...] = jnp.dot(a, w_ref[...].astype(BF16), preferred_element_type=F32) + b_ref[...]


def _mod_table(cond, w_mod, b_mod, tn=1024):
    depth, d, n6 = w_mod.shape
    nc = cond.shape[0]
    tn = math.gcd(n6, tn)
    return pl.pallas_call(
        _mod_kernel,
        grid=(depth, n6 // tn),
        in_specs=[
            pl.BlockSpec((nc, d), lambda l, j: (0, 0)),
            pl.BlockSpec((None, d, tn), lambda l, j: (l, 0, j)),
            pl.BlockSpec((None, 1, tn), lambda l, j: (l, 0, j)),
        ],
        out_specs=pl.BlockSpec((None, nc, tn), lambda l, j: (l, 0, j)),
        out_shape=jax.ShapeDtypeStruct((depth, nc, n6), F32),
        compiler_params=_cparams(("arbitrary", "arbitrary")),
        name="mod_table",
    )(cond, w_mod, b_mod.reshape(depth, 1, n6))


def _norm_mod(x, g, sc, sh):
    ms = jnp.mean(x * x, axis=-1, keepdims=True)
    y = x * lax.rsqrt(ms + EPS) * g
    return y * (1.0 + sc) + sh


def _mod_spec(dm, layer, section, tm):
    return pl.BlockSpec((None, None, 1, dm.D), lambda i: (layer, _cond_idx(dm, i, tm), 0, section))


def _embed_norm_kernel(dm, tm, xs_ref, xp_ref, pos_ref, g_ref, sc_ref, sh_ref, x_ref, u_ref):
    i = pl.program_id(0)
    n_s = dm.Ts // tm

    @pl.when(i < n_s)
    def _():
        x_ref[...] = xs_ref[...] + pos_ref[...]

    @pl.when(i >= n_s)
    def _():
        x_ref[...] = xp_ref[...]

    u_ref[...] = _norm_mod(x_ref[...], g_ref[...], sc_ref[...], sh_ref[...]).astype(u_ref.dtype)


def _embed_norm(dm, xs2, xp2, pos, g, mod4, tm=256):
    n_s = dm.Ts // tm
    n_l = dm.Ls // tm
    row = pl.BlockSpec((tm, dm.D), lambda i: (i, 0))
    return pl.pallas_call(
        functools.partial(_embed_norm_kernel, dm, tm),
        grid=(dm.T // tm,),
        in_specs=[
            pl.BlockSpec((tm, dm.D), lambda i: (jnp.minimum(i, n_s - 1), 0)),
            pl.BlockSpec((tm, dm.D), lambda i: (jnp.maximum(i - n_s, 0), 0)),
            pl.BlockSpec((tm, dm.D), lambda i: (i % n_l, 0)),
            pl.BlockSpec((1, dm.D), lambda i: (0, 0)),
            _mod_spec(dm, 0, 1, tm),
            _mod_spec(dm, 0, 0, tm),
        ],
        out_specs=[row, row],
        out_shape=[jax.ShapeDtypeStruct((dm.T, dm.D), F32), jax.ShapeDtypeStruct((dm.T, dm.D), BF16)],
        compiler_params=_cparams(("arbitrary",)),
        name="embed_norm",
    )(xs2, xp2, pos, g, mod4, mod4)


def _route_rows(s, sel):
    e_total = s.shape[0]
    per = e_total // N_EXPERT_GROUPS
    assert per == 4 and TOP_K == 2
    srow = [s[e:e + 1, :] for e in range(e_total)]
    vrow = [sel[e:e + 1, :] for e in range(e_total)]
    scores = []
    for gi in range(N_EXPERT_GROUPS):
        a, b, c, d = vrow[4 * gi:4 * gi + 4]
        hi1, lo1 = jnp.maximum(a, b), jnp.minimum(a, b)
        hi2, lo2 = jnp.maximum(c, d), jnp.minimum(c, d)
        top = jnp.maximum(hi1, hi2)
        second = jnp.maximum(jnp.minimum(hi1, hi2), jnp.maximum(lo1, lo2))
        scores.append(top + second)
    g = jnp.zeros_like(scores[0])
    best = scores[0]
    for gi in range(1, N_EXPERT_GROUPS):
        upd = scores[gi] > best
        g = jnp.where(upd, float(gi), g)
        best = jnp.where(upd, scores[gi], best)

    def pick(rows, j):
        out = rows[j]
        for gi in range(1, N_EXPERT_GROUPS):
            out = jnp.where(g == float(gi), rows[4 * gi + j], out)
        return out

    v = [pick(vrow, j) for j in range(4)]
    sv = [pick(srow, j) for j in range(4)]
    i1 = jnp.zeros_like(g)
    b1 = v[0]
    for j in range(1, 4):
        upd = v[j] > b1
        i1 = jnp.where(upd, float(j), i1)
        b1 = jnp.where(upd, v[j], b1)
    i2 = jnp.zeros_like(g)
    b2 = jnp.full_like(g, -jnp.inf)
    for j in range(4):
        upd = jnp.logical_and(i1 != float(j), v[j] > b2)
        i2 = jnp.where(upd, float(j), i2)
        b2 = jnp.where(upd, v[j], b2)

    def pick_idx(idx):
        out = sv[0]
        for j in range(1, 4):
            out = jnp.where(idx == float(j), sv[j], out)
        return out

    s1, s2 = pick_idx(i1), pick_idx(i2)
    tot = s1 + s2
    return g * 4.0 + i1, g * 4.0 + i2, s1 / tot, s2 / tot


def _norm_router_kernel(x_ref, g_ref, sc_ref, sh_ref, wr_ref, br_ref, u_ref, r_ref):
    u = _norm_mod(x_ref[...], g_ref[...], sc_ref[...], sh_ref[...])
    u_hi = u.astype(BF16)
    u_ref[...] = _pack_halves(u)
    u_lo = (u - u_hi.astype(F32)).astype(BF16)
    w = wr_ref[...]
    w_hi = w.astype(BF16)
    w_lo = (w - w_hi.astype(F32)).astype(BF16)
    logits = (jnp.dot(u_hi, w_hi, preferred_element_type=F32) + jnp.dot(u_lo, w_hi, preferred_element_type=F32)
              + jnp.dot(u_hi, w_lo, preferred_element_type=F32))
    e_total = br_ref.shape[0]
    lt = logits.T[:e_total, :]
    s = _sigmoid_exp(lt)
    sel = s + br_ref[...]
    e1, e2, w1, w2 = _route_rows(s, sel)
    rows = lax.broadcasted_iota(jnp.int32, r_ref.shape, 0)
    r_ref[...] = jnp.where(rows == 0, e1, jnp.where(rows == 1, e2, jnp.where(rows == 2, w1,
                                                                             jnp.where(rows == 3, w2, 0.0))))


def _norm_router(dm, layer, x, g, mod4, wr_pad, br_col, tm=256):
    row = pl.BlockSpec((tm, dm.D), lambda i: (i, 0))
    return pl.pallas_call(
        _norm_router_kernel,
        grid=(dm.T // tm,),
        in_specs=[
            row,
            pl.BlockSpec((1, dm.D), lambda i: (0, 0)),
            _mod_spec(dm, layer, 4, tm),
            _mod_spec(dm, layer, 3, tm),
            pl.BlockSpec(wr_pad.shape, lambda i: (0, 0)),
            pl.BlockSpec(br_col.shape, lambda i: (0, 0)),
        ],
        out_specs=[pl.BlockSpec((tm, dm.D // 2), lambda i: (i, 0)), pl.BlockSpec((8, tm), lambda i: (0, i))],
        out_shape=[jax.ShapeDtypeStruct((dm.T, dm.D // 2), jnp.uint32), jax.ShapeDtypeStruct((8, dm.T), F32)],
        compiler_params=_cparams(("arbitrary",)),
        name="norm_router",
    )(x, g, mod4, mod4, wr_pad, br_col)


GATHER_UNROLL = 8


def _row_gather_start(src_hbm, idx_ref, idx_base, dst_ref, sem, n_rows, priority):
    def group(g, carry):
        for k in range(GATHER_UNROLL):
            r = g * GATHER_UNROLL + k
            row = idx_ref[idx_base + r]
            pltpu.make_async_copy(src_hbm.at[pl.ds(row, 1), :], dst_ref.at[pl.ds(r, 1), :], sem).start(
                priority=priority)
        return carry

    lax.fori_loop(0, n_rows // GATHER_UNROLL, group, 0)


def _row_gather_wait(src_hbm, dst_ref, sem, n_rows):
    def group(g, carry):
        for k in range(GATHER_UNROLL):
            r = g * GATHER_UNROLL + k
            pltpu.make_async_copy(src_hbm.at[pl.ds(0, 1), :], dst_ref.at[pl.ds(r, 1), :], sem).wait()
        return carry

    lax.fori_loop(0, n_rows // GATHER_UNROLL, group, 0)


def _gathered_moe_rows(s1_ref, s2_ref, ys_hbm, buf, sem, tm, off):
    i = pl.program_id(0)
    n = pl.num_programs(0)
    slot = lax.rem(i, 2)

    def start(step, into):
        base = (step + off) * tm
        _row_gather_start(ys_hbm, s1_ref, base, buf.at[into, 0], sem.at[into], tm, 0)
        _row_gather_start(ys_hbm, s2_ref, base, buf.at[into, 1], sem.at[into], tm, 1)

    @pl.when(i == 0)
    def _():
        start(0, 0)

    @pl.when(i + 1 < n)
    def _():
        start(i + 1, 1 - slot)

    _row_gather_wait(ys_hbm, buf.at[slot, 0], sem.at[slot], tm)
    _row_gather_wait(ys_hbm, buf.at[slot, 1], sem.at[slot], tm)
    return buf[slot, 0], buf[slot, 1]


def _unpack_expert_rows(w, n_split):
    wb = w.shape[1] // n_split
    parts = []
    for j in range(n_split):
        lo, hi = _unpack_halves(w[:, j * wb:(j + 1) * wb])
        parts += [lo, hi]
    return jnp.concatenate(parts, axis=1)


def _moe_residual(s1_ref, s2_ref, ys_hbm, buf, sem, tm, off, n_split, x_ref, w1_ref, w2_ref, g2_ref):
    p1, p2 = _gathered_moe_rows(s1_ref, s2_ref, ys_hbm, buf, sem, tm, off)
    moe = w1_ref[...] * _unpack_expert_rows(p1, n_split) + w2_ref[...] * _unpack_expert_rows(p2, n_split)
    return x_ref[...] + g2_ref[...] * moe


def _combine_norm_kernel(tm, n_split, s1_ref, s2_ref, x_ref, ys_hbm, w1_ref, w2_ref, g2_ref, g_ref, sc_ref, sh_ref,
                         xo_ref, u_ref, buf, sem):
    x = _moe_residual(s1_ref, s2_ref, ys_hbm, buf, sem, tm, 0, n_split, x_ref, w1_ref, w2_ref, g2_ref)
    xo_ref[...] = x
    u_ref[...] = _norm_mod(x, g_ref[...], sc_ref[...], sh_ref[...]).astype(u_ref.dtype)


def _gather_cparams():
    return pltpu.CompilerParams(dimension_semantics=("arbitrary",), vmem_limit_bytes=VMEM_LIMIT,
                                disable_bounds_checks=True)


def _combine_scratch(dm, tm):
    return [pltpu.VMEM((2, TOP_K, tm, dm.D // 2), jnp.uint32), pltpu.SemaphoreType.DMA((2,))]


def _combine_norm(dm, layer, x, ysp, slot1, slot2, w1, w2, g_next, mod4, n_split, tm=256):
    row = pl.BlockSpec((tm, dm.D), lambda i, s1, s2: (i, 0))
    col = pl.BlockSpec((tm, 1), lambda i, s1, s2: (i, 0))
    vec = pl.BlockSpec((1, dm.D), lambda i, s1, s2: (0, 0))

    def mod(lyr, section):
        return pl.BlockSpec((None, None, 1, dm.D), lambda i, s1, s2: (lyr, _cond_idx(dm, i, tm), 0, section))

    return pl.pallas_call(
        functools.partial(_combine_norm_kernel, tm, n_split),
        grid_spec=pltpu.PrefetchScalarGridSpec(
            num_scalar_prefetch=2, grid=(dm.T // tm,),
            in_specs=[row, pl.BlockSpec(memory_space=pl.ANY), col, col, mod(layer, 5), vec,
                      mod(layer + 1, 1), mod(layer + 1, 0)],
            out_specs=[row, row],
            scratch_shapes=_combine_scratch(dm, tm)),
        out_shape=[jax.ShapeDtypeStruct((dm.T, dm.D), F32), jax.ShapeDtypeStruct((dm.T, dm.D), BF16)],
        compiler_params=_gather_cparams(),
        name="combine_norm",
    )(slot1, slot2, x, ysp, w1, w2, mod4, g_next, mod4, mod4)


def _combine_final_kernel(tm, off, n_split, s1_ref, s2_ref, x_ref, ys_hbm, w1_ref, w2_ref, g2_ref, g_ref, o_ref,
                          buf, sem):
    x = _moe_residual(s1_ref, s2_ref, ys_hbm, buf, sem, tm, off, n_split, x_ref, w1_ref, w2_ref, g2_ref)
    ms = jnp.mean(x * x, axis=-1, keepdims=True)
    o_ref[...] = x * lax.rsqrt(ms + EPS) * g_ref[...]


def _combine_final(dm, layer, x, ysp, slot1, slot2, w1, w2, g_final, mod4, n_split, row_start, n_rows, tm=256):
    off = row_start // tm
    row = pl.BlockSpec((tm, dm.D), lambda i, s1, s2: (i + off, 0))
    col = pl.BlockSpec((tm, 1), lambda i, s1, s2: (i + off, 0))
    return pl.pallas_call(
        functools.partial(_combine_final_kernel, tm, off, n_split),
        grid_spec=pltpu.PrefetchScalarGridSpec(
            num_scalar_prefetch=2, grid=(n_rows // tm,),
            in_specs=[row, pl.BlockSpec(memory_space=pl.ANY), col, col,
                      pl.BlockSpec((None, None, 1, dm.D),
                                   lambda i, s1, s2: (layer, _cond_idx(dm, i + off, tm), 0, 5)),
                      pl.BlockSpec((1, dm.D), lambda i, s1, s2: (0, 0))],
            out_specs=pl.BlockSpec((tm, dm.D), lambda i, s1, s2: (i, 0)),
            scratch_shapes=_combine_scratch(dm, tm)),
        out_shape=jax.ShapeDtypeStruct((n_rows, dm.D), F32),
        compiler_params=_gather_cparams(),
        name="combine_final",
    )(slot1, slot2, x, ysp, w1, w2, mod4, g_final)


def _mm_kernel(a_ref, w_ref, o_ref):
    o_ref[...] = jnp.dot(a_ref[...], w_ref[...], preferred_element_type=F32).astype(o_ref.dtype)


def _matmul(a, w, out_dtype, tm, tn, name):
    m, k = a.shape
    n = w.shape[1]
    return pl.pallas_call(
        _mm_kernel,
        grid=(n // tn, m // tm),
        in_specs=[pl.BlockSpec((tm, k), lambda j, i: (i, 0)), pl.BlockSpec((k, tn), lambda j, i: (0, j))],
        out_specs=pl.BlockSpec((tm, tn), lambda j, i: (i, j)),
        out_shape=jax.ShapeDtypeStruct((m, n), out_dtype),
        compiler_params=_cparams(("arbitrary", "arbitrary")),
        name=name,
    )(a, w)


def _merge_kernel(ya_ref, yb_ref, yc_ref, g0_ref, g1_ref, g2_ref, wa_ref, wb_ref, wc_ref, o_ref):
    acc = _sigmoid(g0_ref[...].astype(F32)) * jnp.dot(ya_ref[...], wa_ref[...], preferred_element_type=F32)
    acc += _sigmoid(g1_ref[...].astype(F32)) * jnp.dot(yb_ref[...], wb_ref[...], preferred_element_type=F32)
    acc += _sigmoid(g2_ref[...].astype(F32)) * jnp.dot(yc_ref[...], wc_ref[...], preferred_element_type=F32)
    o_ref[...] = acc.astype(o_ref.dtype)


def _merge(dm, ya, yb, yc, proj, wa, wb, wc, tm=512, tn=1024):
    tn = min(tn, dm.D)
    nb = dm.D // tn
    gb = dm.o_gate // tn

    def gate_spec(k):
        return pl.BlockSpec((tm, tn), lambda j, i: (i, gb + k * nb + j))

    def a_spec(kd):
        return pl.BlockSpec((tm, kd), lambda j, i: (i, 0))

    def w_spec(kd):
        return pl.BlockSpec((kd, tn), lambda j, i: (0, j))

    return pl.pallas_call(
        _merge_kernel,
        grid=(nb, dm.T // tm),
        in_specs=[a_spec(dm.d_lru), a_spec(dm.d_ssd), a_spec(dm.d_pool), gate_spec(0), gate_spec(1), gate_spec(2),
                  w_spec(dm.d_lru), w_spec(dm.d_ssd), w_spec(dm.d_pool)],
        out_specs=pl.BlockSpec((tm, tn), lambda j, i: (i, j)),
        out_shape=jax.ShapeDtypeStruct((dm.T, dm.D), BF16),
        compiler_params=_cparams(("arbitrary", "arbitrary")),
        name="merge",
    )(ya, yb, yc, proj, proj, proj, wa, wb, wc)


def _outproj_kernel(a_ref, w_ref, x_ref, g_ref, o_ref):
    o_ref[...] = x_ref[...] + g_ref[...] * jnp.dot(a_ref[...], w_ref[...], preferred_element_type=F32)


def _outproj(dm, layer, merged, w_out, x, mod4, tm=512, tn=1024):
    tn = min(tn, dm.D)
    nb = dm.D // tn
    return pl.pallas_call(
        _outproj_kernel,
        grid=(nb, dm.T // tm),
        in_specs=[
            pl.BlockSpec((tm, dm.D), lambda j, i: (i, 0)),
            pl.BlockSpec((dm.D, tn), lambda j, i: (0, j)),
            pl.BlockSpec((tm, tn), lambda j, i: (i, j)),
            pl.BlockSpec((None, None, 1, tn), lambda j, i: (layer, _cond_idx(dm, i, tm), 0, 2 * nb + j)),
        ],
        out_specs=pl.BlockSpec((tm, tn), lambda j, i: (i, j)),
        out_shape=jax.ShapeDtypeStruct((dm.T, dm.D), F32),
        compiler_params=_cparams(("arbitrary", "arbitrary")),
        name="outproj",
    )(merged, w_out, x, mod4)


def _halo_specs(dm, tl, width, col_block, tile_of):
    per = tl // HALO
    last = dm.T // HALO - 1
    cur = pl.BlockSpec((tl, width), lambda i: (tile_of(i), col_block))
    prev = pl.BlockSpec((HALO, width), lambda i: (jnp.maximum(tile_of(i) * per - 1, 0), col_block))
    nxt = pl.BlockSpec((HALO, width), lambda i: (jnp.minimum((tile_of(i) + 1) * per, last), col_block))
    return [cur, prev, nxt]


def _fill_ext(ext_ref, cur_ref, prev_ref, next_ref, is_first, is_last, tl):
    ext_ref[pl.ds(HALO, tl), :] = cur_ref[...].astype(F32)
    ext_ref[pl.ds(0, HALO), :] = jnp.where(is_first, 0.0, prev_ref[...].astype(F32))
    ext_ref[pl.ds(HALO + tl, HALO), :] = jnp.where(is_last, 0.0, next_ref[...].astype(F32))


def _conv_from_ext(ext_ref, w_ref, b_ref, tl):
    y = b_ref[...] + ext_ref[pl.ds(HALO, tl), :] * w_ref[CONV_LEFT:CONV_LEFT + 1, :]
    for k in range(CONV_W):
        if k != CONV_LEFT:
            y = y + ext_ref[pl.ds(HALO - CONV_LEFT + k, tl), :] * w_ref[k:k + 1, :]
    return y


def _lru_kernel(dm, tl, reverse, *refs):
    if reverse:
        (xa_ref, xp_ref, xn_ref, cw_ref, cb_ref, wr_ref, br_ref, wi_ref, bi_ref, lam_ref, h0_ref,
         ga_ref, hf_ref, y_ref, hT_ref, ext_ref, a_ref, bx_ref, hs_ref, h_ref) = refs
    else:
        (xa_ref, xp_ref, xn_ref, cw_ref, cb_ref, wr_ref, br_ref, wi_ref, bi_ref, lam_ref, h0_ref,
         y_ref, hT_ref, ext_ref, a_ref, bx_ref, hs_ref, h_ref) = refs
    nt = dm.T // tl
    i = pl.program_id(0)
    tile = nt - 1 - i if reverse else i
    pos, lseq, _ = _seq_tile_info(dm, tile, tl)
    is_first = pos == 0
    is_last = pos + tl == lseq
    _fill_ext(ext_ref, xa_ref, xp_ref, xn_ref, is_first, is_last, tl)
    xc = _conv_from_ext(ext_ref, cw_ref, cb_ref, tl)
    sp = _softplus(-lam_ref[...])
    bw = dm.lru_bw
    for n in range(dm.lru_blocks):
        cs = slice(n * bw, (n + 1) * bw)
        xb = xc[:, cs]
        xbh = xb.astype(BF16)
        r = _sigmoid(jnp.dot(xbh, wr_ref[n], preferred_element_type=F32) + br_ref[:, cs])
        ig = _sigmoid(jnp.dot(xbh, wi_ref[n], preferred_element_type=F32) + bi_ref[:, cs])
        log_a = (-LRU_C) * r * sp[:, cs]
        a = jnp.exp(log_a)
        a_ref[:, cs] = a
        bx_ref[:, cs] = jnp.sqrt(1.0 - a * a) * (ig * xb)

    start_of_scan = is_last if reverse else is_first

    @pl.when(start_of_scan)
    def _():
        h_ref[...] = h0_ref[...]

    def body(t, h):
        tt = tl - 1 - t if reverse else t
        h = a_ref[pl.ds(tt, 1), :] * h + bx_ref[pl.ds(tt, 1), :]
        hs_ref[pl.ds(tt, 1), :] = h
        return h

    h = lax.fori_loop(0, tl, body, h_ref[...], unroll=8)
    h_ref[...] = h
    hT_ref[...] = h
    if reverse:
        y = (hf_ref[...].astype(F32) + hs_ref[...]) * _gelu_tanh(ga_ref[...].astype(F32))
        y_ref[...] = y.astype(y_ref.dtype)
    else:
        y_ref[...] = hs_ref[...].astype(y_ref.dtype)


def _lru(dm, reverse, proj, cw, cb, wr, br, wi, bi, lam, h0, hs_fwd=None, tl=256):
    nt = dm.T // tl
    c = dm.d_lru
    tile_of = (lambda i: nt - 1 - i) if reverse else (lambda i: i)
    seq_of = lambda i: _seq_tile_info(dm, tile_of(i), tl)[2]
    full = lambda shape: pl.BlockSpec(shape, lambda i: (0,) * len(shape))
    row = pl.BlockSpec((tl, c), lambda i: (tile_of(i), 0))
    in_specs = _halo_specs(dm, tl, c, dm.o_xa // c, tile_of) + [
        full((CONV_W, c)), full((1, c)),
        full(wr.shape), full((1, c)), full(wi.shape), full((1, c)), full((1, c)),
        pl.BlockSpec((None, 1, c), lambda i: (seq_of(i), 0, 0)),
    ]
    args = [proj, proj, proj, cw, cb, wr, br, wi, bi, lam, h0]
    if reverse:
        in_specs += [pl.BlockSpec((tl, c), lambda i: (tile_of(i), dm.o_ga // c)), row]
        args += [proj, hs_fwd]
    return pl.pallas_call(
        functools.partial(_lru_kernel, dm, tl, reverse),
        grid=(nt,),
        in_specs=in_specs,
        out_specs=[row, pl.BlockSpec((None, 1, c), lambda i: (seq_of(i), 0, 0))],
        out_shape=[jax.ShapeDtypeStruct((dm.T, c), BF16), jax.ShapeDtypeStruct((dm.nseq, 1, c), F32)],
        scratch_shapes=[pltpu.VMEM((tl + 2 * HALO, c), F32), pltpu.VMEM((tl, c), F32), pltpu.VMEM((tl, c), F32),
                        pltpu.VMEM((tl, c), F32), pltpu.VMEM((1, c), F32)],
        compiler_params=_cparams(("arbitrary",)),
        name="lru_bwd" if reverse else "lru_fwd",
    )(*args)


def _pool_kernel(dm, tl, xc_ref, xp_ref, xn_ref, w_ref, s_ref, o_ref, ext_ref):
    tile = pl.program_id(0)
    pos, lseq, _ = _seq_tile_info(dm, tile, tl)
    _fill_ext(ext_ref, xc_ref, xp_ref, xn_ref, pos == 0, pos + tl == lseq, tl)
    t = pos + lax.broadcasted_iota(jnp.int32, (tl, 1), 0)
    pg = dm.pool_group
    for k, w in enumerate(POOL_WINDOWS):
        cs = slice(k * pg, (k + 1) * pg)
        tok_rows = ext_ref[pl.ds(HALO, tl), cs]
        acc = tok_rows
        for o in range(-w // 2, w // 2):
            if o != 0:
                acc = acc + ext_ref[pl.ds(HALO + o, tl), cs]
        cnt = (jnp.minimum(t + w // 2, lseq) - jnp.maximum(t - w // 2, 0)).astype(F32)
        dev = acc / cnt - tok_rows
        y = jnp.dot(dev.astype(BF16), w_ref[k], preferred_element_type=F32)
        o_ref[:, cs] = (y * s_ref[:, cs]).astype(o_ref.dtype)


def _pool(dm, proj, w_pool, scale, tl=256):
    c = dm.d_pool
    return pl.pallas_call(
        functools.partial(_pool_kernel, dm, tl),
        grid=(dm.T // tl,),
        in_specs=_halo_specs(dm, tl, c, dm.o_xc // c, lambda i: i) + [
            pl.BlockSpec(w_pool.shape, lambda i: (0, 0, 0)), pl.BlockSpec((1, c), lambda i: (0, 0))],
        out_specs=pl.BlockSpec((tl, c), lambda i: (i, 0)),
        out_shape=jax.ShapeDtypeStruct((dm.T, c), BF16),
        scratch_shapes=[pltpu.VMEM((tl + 2 * HALO, c), F32)],
        compiler_params=_cparams(("arbitrary",)),
        name="pool",
    )(proj, proj, proj, w_pool, scale)


def _dot_exact(a, b):
    return jnp.dot(a, b, preferred_element_type=F32, precision=lax.Precision.HIGHEST)


def _ssd_kernel(dm, q, reverse, *refs):
    if reverse:
        (xact_ref, dt_ref, dtb_ref, alog_ref, h0_ref, z_ref, yf_ref, ng_ref,
         y_ref, hT_ref, h_ref, yacc_ref) = refs
    else:
        (xs_ref, xsp_ref, xsn_ref, bc_ref, bcp_ref, bcn_ref, dt_ref, cwx_ref, cbx_ref, cwb_ref, cbb_ref,
         dtb_ref, alog_ref, h0_ref, dsk_ref, y_ref, hT_ref, xact_ref, extx_ref, extb_ref, h_ref) = refs
    nt = dm.T // q
    i = pl.program_id(0)
    tile = nt - 1 - i if reverse else i
    pos, lseq, _ = _seq_tile_info(dm, tile, q)
    is_first = pos == 0
    is_last = pos + q == lseq
    d_ssd, n, g_cnt, r_cnt, p = dm.d_ssd, dm.N, dm.G, dm.R, dm.P
    gn = g_cnt * n
    pair = 2 * p
    assert pair == LANES and n == LANES and q == LANES
    if reverse:
        xs = xact_ref[:, :d_ssd].astype(F32)
        bc = None
    else:
        _fill_ext(extx_ref, xs_ref, xsp_ref, xsn_ref, is_first, is_last, q)
        _fill_ext(extb_ref, bc_ref, bcp_ref, bcn_ref, is_first, is_last, q)
        xs = _silu(_conv_from_ext(extx_ref, cwx_ref, cbx_ref, q))
        bc = _silu(_conv_from_ext(extb_ref, cwb_ref, cbb_ref, q))
        xact_ref[:, :d_ssd] = xs.astype(xact_ref.dtype)
        xact_ref[:, d_ssd:] = bc.astype(xact_ref.dtype)

    @pl.when(is_last if reverse else is_first)
    def _():
        h_ref[...] = h0_ref[...]

    dt = _softplus(dt_ref[...] + dtb_ref[...])
    a_neg = -jnp.exp(alog_ref[...])
    dta = dt * a_neg
    ri = lax.broadcasted_iota(jnp.int32, (q, q), 0)
    ci = lax.broadcasted_iota(jnp.int32, (q, q), 1)
    valid = (ci >= ri) if reverse else (ci <= ri)
    a_cs = _dot_exact(valid.astype(F32), dta)
    a_end = a_cs[0:1, :] if reverse else a_cs[q - 1:q, :]
    w_state = dt * jnp.exp(a_end - a_cs)
    src_t = (a_cs - jnp.log(dt)).T
    cd_t = jnp.broadcast_to(jnp.exp(a_end), (q, LANES)).T
    lane = lax.broadcasted_iota(jnp.int32, (q, LANES), 1)
    low_half = lane < p

    for g in range(g_cnt):
        if reverse:
            b_h = xact_ref[:, d_ssd + g * n:d_ssd + (g + 1) * n]
            c_h = xact_ref[:, d_ssd + gn + g * n:d_ssd + gn + (g + 1) * n]
        else:
            b_h = bc[:, g * n:(g + 1) * n].astype(BF16)
            c_h = bc[:, gn + g * n:gn + (g + 1) * n].astype(BF16)
        scores = lax.dot_general(c_h, b_h, (((1,), (1,)), ((), ())), preferred_element_type=F32)
        r0 = g * r_cnt * p
        h_grp = h_ref[pl.ds(r0, r_cnt * p), :].astype(BF16)
        y_off = lax.dot_general(c_h, h_grp, (((1,), (1,)), ((), ())), preferred_element_type=F32)
        for k in range(r_cnt // 2):
            c0 = r0 + 2 * k * p
            hd0 = g * r_cnt + 2 * k
            xs_pair = xs[:, c0:c0 + pair]
            xs_h = xact_ref[:, c0:c0 + pair] if reverse else xs_pair.astype(BF16)
            ys, cols, ws = [], [], []
            for s in range(2):
                hd = hd0 + s
                col = jnp.broadcast_to(a_cs[:, hd:hd + 1], (q, q))
                row = jnp.broadcast_to(src_t[hd:hd + 1, :], (q, q))
                m = (scores * jnp.exp(jnp.where(valid, col - row, -jnp.inf))).astype(BF16)
                ys.append(jnp.dot(m, xs_h, preferred_element_type=F32))
                cols.append(col)
                ws.append(jnp.broadcast_to(w_state[:, hd:hd + 1], (q, LANES)))
            y_pair = (jnp.where(low_half, ys[0], ys[1])
                      + y_off[:, c0 - r0:c0 - r0 + pair] * jnp.exp(jnp.where(low_half, cols[0], cols[1])))
            xsw = (xs_pair * jnp.where(low_half, ws[0], ws[1])).astype(BF16)
            st = lax.dot_general(xsw, b_h, (((0,), (0,)), ((), ())), preferred_element_type=F32)
            dec = jnp.concatenate([jnp.broadcast_to(cd_t[hd0:hd0 + 1, :], (p, n)),
                                   jnp.broadcast_to(cd_t[hd0 + 1:hd0 + 2, :], (p, n))], axis=0)
            h_ref[pl.ds(c0, pair), :] = dec * h_ref[pl.ds(c0, pair), :] + st
            if reverse:
                yacc_ref[:, c0:c0 + pair] = y_pair
            else:
                y_ref[:, c0:c0 + pair] = (y_pair + dsk_ref[:, c0:c0 + pair] * xs_pair).astype(y_ref.dtype)

    hT_ref[...] = h_ref[...]
    if reverse:
        gw = d_ssd // g_cnt
        for g in range(g_cnt):
            cs = slice(g * gw, (g + 1) * gw)
            y = (yacc_ref[:, cs] + yf_ref[:, cs]) * _silu(z_ref[:, cs].astype(F32))
            ms = jnp.mean(y * y, axis=-1, keepdims=True)
            y_ref[:, cs] = (y * lax.rsqrt(ms + EPS) * ng_ref[:, cs]).astype(y_ref.dtype)


def _ssd(dm, reverse, proj, dtp, cw, cb, dtb, alog, h0, extra, fwd=None, q=128):
    nt = dm.T // q
    tile_of = (lambda i: nt - 1 - i) if reverse else (lambda i: i)
    seq_of = lambda i: _seq_tile_info(dm, tile_of(i), q)[2]
    full = lambda shape: pl.BlockSpec(shape, lambda i: (0,) * len(shape))
    hp = dm.H * dm.P
    rowy = pl.BlockSpec((q, dm.d_ssd), lambda i: (tile_of(i), 0))
    rowact = pl.BlockSpec((q, dm.d_xbc), lambda i: (tile_of(i), 0))
    state = pl.BlockSpec((None, hp, dm.N), lambda i: (seq_of(i), 0, 0))
    dt_spec = pl.BlockSpec((q, LANES), lambda i: (tile_of(i), 1 if reverse else 0))
    vec = [full((1, LANES)), full((1, LANES)), state]
    if reverse:
        y_fwd, xact = fwd
        in_specs = [rowact, dt_spec] + vec + [
            pl.BlockSpec((q, dm.d_ssd), lambda i: (tile_of(i), dm.o_z // dm.d_ssd)), rowy, full((1, dm.d_ssd))]
        args = [xact, dtp, dtb, alog, h0, proj, y_fwd, extra]
        out_specs = [rowy, state]
        out_shape = [jax.ShapeDtypeStruct((dm.T, dm.d_ssd), BF16), jax.ShapeDtypeStruct((dm.nseq, hp, dm.N), F32)]
        scratch = [pltpu.VMEM((hp, dm.N), F32), pltpu.VMEM((q, dm.d_ssd), F32)]
    else:
        in_specs = (_halo_specs(dm, q, dm.d_ssd, dm.o_xs // dm.d_ssd, tile_of)
                    + _halo_specs(dm, q, dm.d_bc, dm.o_bc // dm.d_bc, tile_of)
                    + [dt_spec, full((CONV_W, dm.d_ssd)), full((1, dm.d_ssd)), full((CONV_W, dm.d_bc)),
                       full((1, dm.d_bc))] + vec + [full((1, dm.d_ssd))])
        args = [proj] * 6 + [dtp, cw[:, :dm.d_ssd], cb[:, :dm.d_ssd], cw[:, dm.d_ssd:], cb[:, dm.d_ssd:],
                             dtb, alog, h0, extra]
        out_specs = [rowy, state, rowact]
        out_shape = [jax.ShapeDtypeStruct((dm.T, dm.d_ssd), F32), jax.ShapeDtypeStruct((dm.nseq, hp, dm.N), F32),
                     jax.ShapeDtypeStruct((dm.T, dm.d_xbc), BF16)]
        scratch = [pltpu.VMEM((q + 2 * HALO, dm.d_ssd), F32), pltpu.VMEM((q + 2 * HALO, dm.d_bc), F32),
                   pltpu.VMEM((hp, dm.N), F32)]
    return pl.pallas_call(
        functools.partial(_ssd_kernel, dm, q, reverse),
        grid=(nt,),
        in_specs=in_specs,
        out_specs=out_specs,
        out_shape=out_shape,
        scratch_shapes=scratch,
        compiler_params=_cparams(("arbitrary",)),
        name="ssd_bwd" if reverse else "ssd_fwd",
    )(*args)


def _moe_dispatch_kernel(tm, n_exp, s1_ref, s2_ref, lo_ref, hi_ref, u_ref, o_hbm, zrow, sem, zsem):
    i = pl.program_id(0)
    base = i * tm

    def start_group(g, carry):
        for k in range(GATHER_UNROLL):
            r = g * GATHER_UNROLL + k
            src = u_ref.at[pl.ds(r, 1), :]
            pltpu.make_async_copy(src, o_hbm.at[pl.ds(s1_ref[base + r], 1), :], sem).start(priority=0)
            pltpu.make_async_copy(src, o_hbm.at[pl.ds(s2_ref[base + r], 1), :], sem).start(priority=1)
        return carry

    def wait_group(g, carry):
        for k in range(GATHER_UNROLL):
            r = g * GATHER_UNROLL + k
            src = u_ref.at[pl.ds(r, 1), :]
            pltpu.make_async_copy(src, o_hbm.at[pl.ds(0, 1), :], sem).wait()
            pltpu.make_async_copy(src, o_hbm.at[pl.ds(0, 1), :], sem).wait()
        return carry

    lax.fori_loop(0, tm // GATHER_UNROLL, start_group, 0)
    lax.fori_loop(0, tm // GATHER_UNROLL, wait_group, 0)

    @pl.when(i == pl.num_programs(0) - 1)
    def _():
        zrow[...] = jnp.zeros_like(zrow)
        for e in range(n_exp):
            lo = lo_ref[e]
            hi = hi_ref[e]

            def zstart(r, carry):
                pltpu.make_async_copy(zrow, o_hbm.at[pl.ds(r, 1), :], zsem).start()
                return carry

            def zwait(r, carry):
                pltpu.make_async_copy(zrow, o_hbm.at[pl.ds(0, 1), :], zsem).wait()
                return carry

            lax.fori_loop(lo, hi, zstart, 0)
            lax.fori_loop(lo, hi, zwait, 0)


def _moe_dispatch(u2p, slot1, slot2, pad_lo, pad_hi, p_pad, tm):
    t, words = u2p.shape
    n_exp = pad_lo.shape[0]
    return pl.pallas_call(
        functools.partial(_moe_dispatch_kernel, tm, n_exp),
        grid_spec=pltpu.PrefetchScalarGridSpec(
            num_scalar_prefetch=4, grid=(t // tm,),
            in_specs=[pl.BlockSpec((tm, words), lambda i, s1, s2, lo, hi: (i, 0))],
            out_specs=pl.BlockSpec(memory_space=pl.ANY),
            scratch_shapes=[pltpu.VMEM((1, words), jnp.uint32), pltpu.SemaphoreType.DMA(()),
                            pltpu.SemaphoreType.DMA(())]),
        out_shape=jax.ShapeDtypeStruct((p_pad, words), jnp.uint32),
        compiler_params=_gather_cparams(),
        name="moe_dispatch",
    )(slot1, slot2, pad_lo, pad_hi, u2p)


def _expert_changed(te_ref, i):
    return jnp.logical_or(i == 0, te_ref[i] != te_ref[jnp.maximum(i - 1, 0)])


def _expert_up_kernel(te_ref, nu_ref, x_ref, wg_ref, wu_ref, h_ref, wg_s, wu_s):
    i = pl.program_id(1)

    @pl.when(jnp.logical_and(i < nu_ref[0], _expert_changed(te_ref, i)))
    def _():
        wg_s[...] = wg_ref[...].astype(BF16)
        wu_s[...] = wu_ref[...].astype(BF16)

    @pl.when(i < nu_ref[0])
    def _():
        lo, hi = _unpack_halves(x_ref[...])
        lo = lo.astype(BF16)
        hi = hi.astype(BF16)
        half = lo.shape[1]
        a = (jnp.dot(lo, wg_s[:half, :], preferred_element_type=F32)
             + jnp.dot(hi, wg_s[half:, :], preferred_element_type=F32))
        b = (jnp.dot(lo, wu_s[:half, :], preferred_element_type=F32)
             + jnp.dot(hi, wu_s[half:, :], preferred_element_type=F32))
        h_ref[...] = (_silu(a) * b).astype(h_ref.dtype)

    @pl.when(i >= nu_ref[0])
    def _():
        h_ref[...] = jnp.zeros_like(h_ref)


def _expert_down_kernel(te_ref, nu_ref, h_ref, wd_ref, y_ref, wd_s):
    i = pl.program_id(1)

    @pl.when(jnp.logical_and(i < nu_ref[0], _expert_changed(te_ref, i)))
    def _():
        wd_s[...] = wd_ref[...].astype(BF16)

    @pl.when(i < nu_ref[0])
    def _():
        y_ref[...] = _pack_halves(jnp.dot(h_ref[...], wd_s[...], preferred_element_type=F32))

    @pl.when(i >= nu_ref[0])
    def _():
        y_ref[...] = jnp.zeros_like(y_ref)


def _experts(dm, layer, xsp, w_gate, w_up, w_down, tile_expert, n_used, tm, up_split, down_split):
    p_pad = xsp.shape[0]
    n_tiles = p_pad // tm
    d, f = dm.D, dm.F
    fh, dh = f // up_split, d // down_split
    h = pl.pallas_call(
        _expert_up_kernel,
        grid_spec=pltpu.PrefetchScalarGridSpec(
            num_scalar_prefetch=2, grid=(up_split, n_tiles),
            in_specs=[pl.BlockSpec((tm, d // 2), lambda j, i, te, nu: (i, 0)),
                      pl.BlockSpec((None, None, d, fh), lambda j, i, te, nu: (layer, te[i], 0, j)),
                      pl.BlockSpec((None, None, d, fh), lambda j, i, te, nu: (layer, te[i], 0, j))],
            out_specs=pl.BlockSpec((tm, fh), lambda j, i, te, nu: (i, j)),
            scratch_shapes=[pltpu.VMEM((d, fh), BF16), pltpu.VMEM((d, fh), BF16)]),
        out_shape=jax.ShapeDtypeStruct((p_pad, f), BF16),
        compiler_params=_cparams(("arbitrary", "arbitrary")),
        name="expert_up",
    )(tile_expert, n_used, xsp, w_gate, w_up)
    return pl.pallas_call(
        _expert_down_kernel,
        grid_spec=pltpu.PrefetchScalarGridSpec(
            num_scalar_prefetch=2, grid=(down_split, n_tiles),
            in_specs=[pl.BlockSpec((tm, f), lambda j, i, te, nu: (i, 0)),
                      pl.BlockSpec((None, None, f, dh), lambda j, i, te, nu: (layer, te[i], 0, j))],
            out_specs=pl.BlockSpec((tm, dh // 2), lambda j, i, te, nu: (i, j)),
            scratch_shapes=[pltpu.VMEM((f, dh), BF16)]),
        out_shape=jax.ShapeDtypeStruct((p_pad, d // 2), jnp.uint32),
        compiler_params=_cparams(("arbitrary", "arbitrary")),
        name="expert_down",
    )(tile_expert, n_used, h, w_down)


def _dispatch_plan(dm, route, tm):
    t = dm.T
    e_cnt = dm.E
    ids = jnp.arange(e_cnt, dtype=jnp.int32)[None, :]
    first = route[0].astype(jnp.int32)[:, None] == ids
    second = route[1].astype(jnp.int32)[:, None] == ids
    uses = jnp.logical_or(first, second).astype(jnp.int32)
    before = jnp.cumsum(uses, axis=0) - uses
    counts = jnp.sum(uses, axis=0)
    tiles_per = (counts + tm - 1) // tm
    tile_end = jnp.cumsum(tiles_per)
    start = (tile_end - tiles_per) * tm
    row = start[None, :] + before
    slot1 = jnp.sum(jnp.where(first, row, 0), axis=1)
    slot2 = jnp.sum(jnp.where(second, row, 0), axis=1)
    n_tiles = (TOP_K * t) // tm + e_cnt
    p_pad = n_tiles * tm
    tile_ids = jnp.arange(n_tiles, dtype=jnp.int32)
    tile_expert = jnp.minimum(jnp.sum((tile_end[None, :] <= tile_ids[:, None]).astype(jnp.int32), axis=1),
                              e_cnt - 1)
    n_used = tile_end[-1:].astype(jnp.int32)
    pad_lo = jnp.concatenate([start + counts, tile_end[-1:] * tm]).astype(jnp.int32)
    pad_hi = jnp.concatenate([tile_end * tm, jnp.full((1,), p_pad, tile_end.dtype)]).astype(jnp.int32)
    return p_pad, pad_lo, pad_hi, tile_expert, n_used, slot1, slot2


def _grid_pos_embed(n_tokens, d_model):
    rows = n_tokens // GRID_W
    row = jnp.repeat(jnp.arange(rows), GRID_W).astype(F32)
    col = jnp.tile(jnp.arange(GRID_W), rows).astype(F32)
    quarter = d_model // 4
    omega = 1.0 / (10000.0 ** (jnp.arange(quarter, dtype=F32) / quarter))
    ang_r = row[:, None] * omega[None, :]
    ang_c = col[:, None] * omega[None, :]
    return jnp.concatenate([jnp.sin(ang_r), jnp.cos(ang_r), jnp.sin(ang_c), jnp.cos(ang_c)], axis=-1)


def _pad_lanes(v, fill=0.0):
    return jnp.pad(v.astype(F32), (0, LANES - v.shape[0]), constant_values=fill).reshape(1, LANES)


def kernel(x_prompt, x_sample, state_lru, state_ssd, c, c_ctx, w_mod, b_mod, norm1_g, w_in, conv_a_w, conv_a_b, w_r, b_r, w_i, b_i, lru_lambda, w_oa, conv_b_w, conv_b_b, dt_bias, a_log, d_skip, ssd_norm_g, w_ob, w_pool, pool_scale, w_oc, w_out, norm2_g, w_router, b_router, w_gate, w_up, w_down, final_g):
    dm = Dims(x_prompt, x_sample, state_ssd, w_r, w_pool, w_gate, dt_bias, conv_b_w, w_oa, w_ob, w_oc)
    d = dm.D
    moe_tm = 512
    assert dm.T % moe_tm == 0
    up_split = 4 if dm.F % (4 * 2 * LANES) == 0 else 1
    down_split = 2 if d % (2 * 2 * LANES) == 0 else 1

    ncond = 16
    cond = jnp.zeros((ncond, d), F32).at[:dm.Bs].set(c).at[dm.Bs].set(c_ctx)
    mod4 = _mod_table(cond, w_mod, b_mod).reshape(dm.depth, ncond, 1, 6 * d)

    pos = _grid_pos_embed(dm.Ls, d)
    x, u = _embed_norm(dm, x_sample.reshape(dm.Ts, d), x_prompt.reshape(dm.Tp, d), pos,
                       norm1_g[0].reshape(1, d), mod4)

    wr_pad = jnp.pad(w_router, ((0, 0), (0, LANES - dm.E)))
    br_col = b_router.reshape(dm.E, 1).astype(F32)
    hp = dm.H * dm.P
    lru_states, ssd_states = [], []
    out = None
    for l in range(dm.depth):
        wl = w_in[l]
        w_main = jnp.concatenate([wl[:, :dm.o_dt], wl[:, dm.o_dt + 2 * dm.H:]], axis=1).astype(BF16)
        lane_pad = ((0, 0), (0, LANES - dm.H))
        w_dt = jnp.concatenate([jnp.pad(wl[:, dm.o_dt:dm.o_dt + dm.H], lane_pad),
                                jnp.pad(wl[:, dm.o_dt + dm.H:dm.o_dt + 2 * dm.H], lane_pad)], axis=1).astype(BF16)
        proj = _matmul(u, w_main, BF16, min(1024, dm.Lp), min(1024, dm.d_lru), "in_proj")
        dtp = _matmul(u, w_dt, F32, min(1024, dm.Lp), 2 * LANES, "dt_proj")

        h0_lru = [jnp.concatenate([state_lru[:, l, dd], jnp.zeros((dm.Bp, dm.d_lru), F32)], axis=0)
                  .reshape(dm.nseq, 1, dm.d_lru) for dd in range(2)]
        lru_args = lambda dd: (conv_a_w[l], conv_a_b[l].reshape(1, -1), w_r[l, dd].astype(BF16),
                               b_r[l, dd].reshape(1, -1), w_i[l, dd].astype(BF16), b_i[l, dd].reshape(1, -1),
                               lru_lambda[l, dd].reshape(1, -1), h0_lru[dd])
        hs_f, lru_tf = _lru(dm, False, proj, *lru_args(0))
        ya, lru_tb = _lru(dm, True, proj, *lru_args(1), hs_fwd=hs_f)
        lru_states.append(jnp.stack([lru_tf[dm.Bs:, 0], lru_tb[dm.Bs:, 0]], axis=1))

        h0_ssd = [jnp.concatenate([state_ssd[:, l, dd].reshape(dm.Bs, hp, dm.N),
                                   jnp.zeros((dm.Bp, hp, dm.N), F32)], axis=0) for dd in range(2)]
        ssd_args = lambda dd: (conv_b_w[l], conv_b_b[l].reshape(1, -1), _pad_lanes(dt_bias[l, dd]),
                               _pad_lanes(a_log[l, dd]), h0_ssd[dd])
        dsk = jnp.repeat(d_skip[l].astype(F32), dm.P).reshape(1, dm.d_ssd)
        y_f, ssd_tf, xact = _ssd(dm, False, proj, dtp, *ssd_args(0), dsk)
        yb, ssd_tb = _ssd(dm, True, proj, dtp, *ssd_args(1), ssd_norm_g[l].reshape(1, -1), fwd=(y_f, xact))
        ssd_states.append(jnp.stack([ssd_tf[dm.Bs:], ssd_tb[dm.Bs:]], axis=1)
                          .reshape(dm.Bp, 2, dm.H, dm.P, dm.N))

        yc = _pool(dm, proj, w_pool[l].astype(BF16), pool_scale[l].reshape(1, -1))

        merged = _merge(dm, ya, yb, yc, proj, w_oa[l].astype(BF16), w_ob[l].astype(BF16),
                        w_oc[l].astype(BF16), tm=min(512, dm.Lp))
        x = _outproj(dm, l, merged, w_out[l].astype(BF16), x, mod4, tm=min(512, dm.Lp))

        u2p, route = _norm_router(dm, l, x, norm2_g[l].reshape(1, d), mod4, wr_pad, br_col)
        p_pad, pad_lo, pad_hi, tile_expert, n_used, slot1, slot2 = _dispatch_plan(dm, route, moe_tm)
        n_split = down_split
        xsp = _moe_dispatch(u2p, slot1, slot2, pad_lo, pad_hi, p_pad, min(256, dm.Lp))
        ysp = _experts(dm, l, xsp, w_gate, w_up, w_down, tile_expert, n_used, moe_tm, up_split, n_split)
        w1 = route[2].reshape(dm.T, 1)
        w2 = route[3].reshape(dm.T, 1)
        if l + 1 < dm.depth:
            x, u = _combine_norm(dm, l, x, ysp, slot1, slot2, w1, w2, norm1_g[l + 1].reshape(1, d), mod4, n_split)
        else:
            fg = final_g.reshape(1, d)
            out = (_combine_final(dm, l, x, ysp, slot1, slot2, w1, w2, fg, mod4, n_split, 0, dm.Ts),
                   _combine_final(dm, l, x, ysp, slot1, slot2, w1, w2, fg, mod4, n_split, dm.Ts, dm.Tp))

    y_sample = out[0].reshape(dm.Bs, dm.Ls, d)
    y_prompt = out[1].reshape(dm.Bp, dm.Lp, d)
    new_state_lru = jnp.stack(lru_states, axis=1).astype(x_prompt.dtype)
    new_state_ssd = jnp.stack(ssd_states, axis=1).astype(x_prompt.dtype)
    return (y_prompt, y_sample, new_state_lru, new_state_ssd)
```
